```python
import math
import jax, jax.numpy as jnp
from jax import lax
import numpy as np

D_MODEL = 1024
BATCH = 1
SEQ = 16384
DEPTH = 4

GRID_W = 64
CTX_LEN = 256
EPS = 1e-6
Q_BLOCK = 128
ROPE_BASE = 10000.0

NA_HEADS = 8
NA_HD = 64
NA_W = NA_HEADS * NA_HD
WIN_R = 8
WIN_C = 16

MLA_HEADS = 8
MLA_NOPE = 64
MLA_ROPE = 32
MLA_V = 64
MLA_QK = MLA_NOPE + MLA_ROPE
MLA_W = MLA_HEADS * MLA_V
Q_LORA = 384
KV_LORA = 256

DIFF_HEADS = 4
DIFF_HD = 64
DIFF_W = DIFF_HEADS * 2 * DIFF_HD

LRU_W = 512
LRU_BLOCKS = 8
LRU_BW = LRU_W // LRU_BLOCKS
CONV_W = 4
LRU_C = 8.0

EVEN_SPLITS = (NA_W, NA_W, NA_W, NA_W, Q_LORA, KV_LORA, MLA_ROPE, MLA_W)
EVEN_IN = sum(EVEN_SPLITS)
EVEN_MIX = NA_W + MLA_W
ODD_SPLITS = (DIFF_W, DIFF_W, DIFF_W, DIFF_W, LRU_W, LRU_W)
ODD_IN = sum(ODD_SPLITS)
ODD_MIX = DIFF_W + LRU_W
N_EVEN = (DEPTH + 1) // 2
N_ODD = DEPTH // 2

kernel_name = 'hybrid_na_mla_diff_rglru_prefix_dit'


def _rmsnorm(x, g):
    xf = x.astype(jnp.float32)
    y = xf * lax.rsqrt(jnp.mean(xf * xf, axis=-1, keepdims=True) + EPS)
    return (y * g.astype(jnp.float32)).astype(x.dtype)


def _split(t, sizes):
    bounds = np.cumsum(sizes)[:-1].tolist()
    return jnp.split(t, bounds, axis=-1)


def _heads(t, n_heads):
    b, n, _ = t.shape
    return t.reshape(b, n, n_heads, -1).transpose(0, 2, 1, 3)


def _merge(t):
    b, h, n, d = t.shape
    return t.transpose(0, 2, 1, 3).reshape(b, n, h * d)


def _axial_angles(n_tok, rot_dim):
    t = jnp.arange(n_tok)
    row = (t // GRID_W).astype(jnp.float32)
    col = (t % GRID_W).astype(jnp.float32)
    n_freq = rot_dim // 4
    inv = ROPE_BASE ** (-jnp.arange(n_freq, dtype=jnp.float32) / n_freq)
    ang = jnp.concatenate([row[:, None] * inv, col[:, None] * inv], axis=-1)
    return jnp.cos(ang), jnp.sin(ang)


def _apply_rope(x, rope):
    cos, sin = rope
    half = x.shape[-1] // 2
    x1 = x[..., :half].astype(jnp.float32)
    x2 = x[..., half:].astype(jnp.float32)
    return jnp.concatenate([x1 * cos - x2 * sin, x1 * sin + x2 * cos], axis=-1).astype(x.dtype)


def _sweep(fn, qs):
    b, h, n, _ = qs[0].shape
    nb = n // Q_BLOCK
    blocks = tuple(q.reshape(b, h, nb, Q_BLOCK, q.shape[-1]).transpose(2, 0, 1, 3, 4) for q in qs)
    out = lax.map(fn, blocks)
    return out.transpose(1, 2, 0, 3, 4).reshape(b, h, n, out.shape[-1])


def _softmax_attn(q, k, v, scale):
    def block(qb):
        (qb,) = qb
        s = jnp.einsum('bhqd,bhkd->bhqk', qb, k, preferred_element_type=jnp.float32) * scale
        p = jax.nn.softmax(s, axis=-1).astype(v.dtype)
        return jnp.einsum('bhqk,bhkd->bhqd', p, v)
    return _sweep(block, (q,))


def _diff_attn(q1, q2, k1, k2, v, lam, scale):
    def block(qb):
        a, b = qb
        s1 = jnp.einsum('bhqd,bhkd->bhqk', a, k1, preferred_element_type=jnp.float32) * scale
        s2 = jnp.einsum('bhqd,bhkd->bhqk', b, k2, preferred_element_type=jnp.float32) * scale
        p = jax.nn.softmax(s1, axis=-1) - lam * jax.nn.softmax(s2, axis=-1)
        return jnp.einsum('bhqk,bhkd->bhqd', p.astype(v.dtype), v)
    return _sweep(block, (q1, q2))


def _neighbourhood_attn(q, k, v, k_ctx, v_ctx, rpb, scale):
    b, h, n, d = q.shape
    rows = n // GRID_W
    kr = min(WIN_R, rows)
    kc = min(WIN_C, GRID_W)
    qg = q.reshape(b, h, rows, GRID_W, d)
    kg = k.reshape(b, h, rows, GRID_W, d)
    vg = v.reshape(b, h, rows, GRID_W, d)
    col = jnp.arange(GRID_W)
    c0 = jnp.clip(col - kc // 2, 0, GRID_W - kc)
    col_idx = c0[:, None] + jnp.arange(kc)[None, :]
    col_off = col_idx - col[:, None] + (WIN_C - 1)

    def row_fn(r):
        r0 = jnp.clip(r - kr // 2, 0, rows - kr)
        qr = lax.dynamic_index_in_dim(qg, r, axis=2, keepdims=False)
        k_rows = lax.dynamic_slice_in_dim(kg, r0, kr, axis=2)
        v_rows = lax.dynamic_slice_in_dim(vg, r0, kr, axis=2)
        k_win = k_rows[:, :, :, col_idx]
        v_win = v_rows[:, :, :, col_idx]
        row_off = r0 + jnp.arange(kr) - r + (WIN_R - 1)
        bias = rpb[:, row_off][:, :, col_off]
        s_lat = jnp.einsum('bhwd,bhawjd->bhwaj', qr, k_win, preferred_element_type=jnp.float32) * scale
        s_lat = s_lat + bias.transpose(0, 2, 1, 3).astype(jnp.float32)
        s_ctx = jnp.einsum('bhwd,bhnd->bhwn', qr, k_ctx, preferred_element_type=jnp.float32) * scale
        s = jnp.concatenate([s_lat.reshape(b, h, GRID_W, kr * kc), s_ctx], axis=-1)
        p = jax.nn.softmax(s, axis=-1).astype(v.dtype)
        p_lat = p[..., :kr * kc].reshape(b, h, GRID_W, kr, kc)
        p_ctx = p[..., kr * kc:]
        return (jnp.einsum('bhwaj,bhawjd->bhwd', p_lat, v_win)
                + jnp.einsum('bhwn,bhnd->bhwd', p_ctx, v_ctx))

    out = lax.map(row_fn, jnp.arange(rows))
    return out.transpose(1, 2, 0, 3, 4).reshape(b, h, n, d)


def _mla_q(p_q, q_norm, w_uq, rope):
    b, n, _ = p_q.shape
    q = (_rmsnorm(p_q, q_norm) @ w_uq).reshape(b, n, MLA_HEADS, MLA_QK).transpose(0, 2, 1, 3)
    if rope is None:
        return q
    return jnp.concatenate([q[..., :MLA_NOPE], _apply_rope(q[..., MLA_NOPE:], rope)], axis=-1)


def _mla_kv(p_kv, p_kr, kv_norm, w_ukv, rope):
    b, n, _ = p_kv.shape
    kv = (_rmsnorm(p_kv, kv_norm) @ w_ukv).reshape(b, n, MLA_HEADS, MLA_NOPE + MLA_V).transpose(0, 2, 1, 3)
    k_rope = p_kr[:, None]
    if rope is not None:
        k_rope = _apply_rope(k_rope, rope)
    k = jnp.concatenate([kv[..., :MLA_NOPE], jnp.broadcast_to(k_rope, (b, MLA_HEADS, n, MLA_ROPE))], axis=-1)
    return k, kv[..., MLA_NOPE:]


def _diff_qk(t, rope):
    b, n, _ = t.shape
    th = t.reshape(b, n, DIFF_HEADS, 2, DIFF_HD).transpose(3, 0, 2, 1, 4)
    if rope is not None:
        th = _apply_rope(th, rope)
    return th[0], th[1]


def _diff_out(o, subln, lam_init):
    return _merge(_rmsnorm(o, subln) * (1.0 - lam_init))


def _gated_out(y, g, w_out):
    return (y * jax.nn.silu(g)) @ w_out


def _dwconv(x, w, bias):
    n = x.shape[1]
    left = CONV_W // 2
    xp = jnp.pad(x, ((0, 0), (left, CONV_W - 1 - left), (0, 0)))
    y = bias
    for j in range(CONV_W):
        y = y + xp[:, j:j + n] * w[j]
    return y


def _blockdiag(x, w, bias):
    b, n, c = x.shape
    xb = x.reshape(b, n, LRU_BLOCKS, LRU_BW)
    return jnp.einsum('bngi,gij->bngj', xb, w).reshape(b, n, c) + bias


def _rglru_coeffs(u, lam, wa, ba, wx, bx):
    uf = u.astype(jnp.float32)
    r = jax.nn.sigmoid(_blockdiag(uf, wa.astype(jnp.float32), ba.astype(jnp.float32)))
    i = jax.nn.sigmoid(_blockdiag(uf, wx.astype(jnp.float32), bx.astype(jnp.float32)))
    log_a = -LRU_C * r * jax.nn.softplus(-lam.astype(jnp.float32))
    return jnp.exp(log_a), jnp.sqrt(-jnp.expm1(2.0 * log_a)) * (i * uf)


def _linear_scan(a, bx, h0, reverse):
    if h0 is not None:
        if reverse:
            bx = bx.at[:, -1].add(a[:, -1] * h0)
        else:
            bx = bx.at[:, 0].add(a[:, 0] * h0)

    def comb(e1, e2):
        a1, b1 = e1
        a2, b2 = e2
        return a1 * a2, a2 * b1 + b2

    _, h = lax.associative_scan(comb, (a, bx), axis=1, reverse=reverse)
    return h


def _bi_rglru(u_l, u_c, lam, wa, ba, wx, bx, ctx_out):
    hs_l, hs_c = [], []
    for d, rev in enumerate((False, True)):
        a_c, b_c = _rglru_coeffs(u_c, lam[d], wa[d], ba[d], wx[d], bx[d])
        h_c = _linear_scan(a_c, b_c, None, rev)
        h_end = h_c[:, 0] if rev else h_c[:, -1]
        a_l, b_l = _rglru_coeffs(u_l, lam[d], wa[d], ba[d], wx[d], bx[d])
        hs_l.append(_linear_scan(a_l, b_l, h_end, rev))
        hs_c.append(h_c)
    y_l = (hs_l[0] + hs_l[1]).astype(u_l.dtype)
    if not ctx_out:
        return y_l, None
    return y_l, (hs_c[0] + hs_c[1]).astype(u_c.dtype)


def _even_mixer(hl, hc, w_in, w_out, rpb, q_norm, w_uq, kv_norm, w_ukv, ctx_out):
    pl = _split(hl @ w_in, EVEN_SPLITS)
    pc = _split(hc @ w_in, EVEN_SPLITS)
    rope = _axial_angles(hl.shape[1], MLA_ROPE)
    na_scale = NA_HD ** -0.5
    mla_scale = MLA_QK ** -0.5
    qa_l, ka_l, va_l = [_heads(t, NA_HEADS) for t in pl[:3]]
    ka_c, va_c = [_heads(t, NA_HEADS) for t in pc[1:3]]
    o_na_l = _neighbourhood_attn(qa_l, ka_l, va_l, ka_c, va_c, rpb, na_scale)
    kb_c, vb_c = _mla_kv(pc[5], pc[6], kv_norm, w_ukv, None)
    kb_l, vb_l = _mla_kv(pl[5], pl[6], kv_norm, w_ukv, rope)
    qb_l = _mla_q(pl[4], q_norm, w_uq, rope)
    o_mla_l = _softmax_attn(qb_l, jnp.concatenate([kb_c, kb_l], axis=2),
                            jnp.concatenate([vb_c, vb_l], axis=2), mla_scale)
    y_l = _gated_out(jnp.concatenate([_merge(o_na_l), _merge(o_mla_l)], axis=-1),
                     jnp.concatenate([pl[3], pl[7]], axis=-1), w_out)
    if not ctx_out:
        return y_l, None
    o_na_c = _softmax_attn(_heads(pc[0], NA_HEADS), ka_c, va_c, na_scale)
    o_mla_c = _softmax_attn(_mla_q(pc[4], q_norm, w_uq, None), kb_c, vb_c, mla_scale)
    y_c = _gated_out(jnp.concatenate([_merge(o_na_c), _merge(o_mla_c)], axis=-1),
                     jnp.concatenate([pc[3], pc[7]], axis=-1), w_out)
    return y_l, y_c


def _odd_mixer(hl, hc, w_in, w_out, lq1, lk1, lq2, lk2, subln, conv_w, conv_b,
               lam, wa, ba, wx, bx, lam_init, ctx_out):
    pl = _split(hl @ w_in, ODD_SPLITS)
    pc = _split(hc @ w_in, ODD_SPLITS)
    rope = _axial_angles(hl.shape[1], DIFF_HD)
    scale = DIFF_HD ** -0.5
    f32 = jnp.float32
    lam_d = (jnp.exp(jnp.sum(lq1.astype(f32) * lk1.astype(f32)))
             - jnp.exp(jnp.sum(lq2.astype(f32) * lk2.astype(f32))) + lam_init)
    q1_l, q2_l = _diff_qk(pl[0], rope)
    k1_l, k2_l = _diff_qk(pl[1], rope)
    k1_c, k2_c = _diff_qk(pc[1], None)
    v_l = _heads(pl[2], DIFF_HEADS)
    v_c = _heads(pc[2], DIFF_HEADS)
    o_diff_l = _diff_attn(q1_l, q2_l, jnp.concatenate([k1_c, k1_l], axis=2),
                          jnp.concatenate([k2_c, k2_l], axis=2),
                          jnp.concatenate([v_c, v_l], axis=2), lam_d, scale)
    u_l = _dwconv(pl[4], conv_w, conv_b)
    u_c = _dwconv(pc[4], conv_w, conv_b)
    h_l, h_c = _bi_rglru(u_l, u_c, lam, wa, ba, wx, bx, ctx_out)
    y_l = _gated_out(jnp.concatenate([_diff_out(o_diff_l, subln, lam_init), h_l], axis=-1),
                     jnp.concatenate([pl[3], pl[5]], axis=-1), w_out)
    if not ctx_out:
        return y_l, None
    q1_c, q2_c = _diff_qk(pc[0], None)
    o_diff_c = _diff_attn(q1_c, q2_c, k1_c, k2_c, v_c, lam_d, scale)
    y_c = _gated_out(jnp.concatenate([_diff_out(o_diff_c, subln, lam_init), h_c], axis=-1),
                     jnp.concatenate([pc[3], pc[5]], axis=-1), w_out)
    return y_l, y_c


def setup_inputs(seed: int = 0) -> dict:
    key = jax.random.key(seed)
    keys = jax.random.split(key, 32)

    def nrm(i, shape, s):
        return s * jax.random.normal(keys[i], shape, jnp.float32)

    E, O, D = N_EVEN, N_ODD, D_MODEL
    u = jax.random.uniform(keys[24], (O, 2, LRU_W), jnp.float32, 0.9, 0.999)
    a = u ** (1.0 / LRU_C)
    return {
        'x': nrm(0, (BATCH, SEQ, D), 1.0),
        'c': nrm(1, (BATCH, D), 1.0),
        'ctx': nrm(2, (BATCH, CTX_LEN, D), 1.0),
        'c_ctx': nrm(3, (D,), 1.0),
        'mod_w': nrm(4, (DEPTH, D, 3 * D), 0.5 * D ** -0.5),
        'mod_b': nrm(5, (DEPTH, 3 * D), 0.02),
        'norm_g': 1.0 + nrm(6, (DEPTH, D), 0.01),
        'final_g': 1.0 + nrm(7, (D,), 0.01),
        'e_w_in': nrm(8, (E, D, EVEN_IN), D ** -0.5),
        'e_w_out': nrm(9, (E, EVEN_MIX, D), EVEN_MIX ** -0.5),
        'na_rpb': nrm(10, (E, NA_HEADS, 2 * WIN_R - 1, 2 * WIN_C - 1), 0.1),
        'mla_q_norm': 1.0 + nrm(11, (E, Q_LORA), 0.01),
        'mla_w_uq': nrm(12, (E, Q_LORA, MLA_HEADS * MLA_QK), Q_LORA ** -0.5),
        'mla_kv_norm': 1.0 + nrm(13, (E, KV_LORA), 0.01),
        'mla_w_ukv': nrm(14, (E, KV_LORA, MLA_HEADS * (MLA_NOPE + MLA_V)), KV_LORA ** -0.5),
        'o_w_in': nrm(15, (O, D, ODD_IN), D ** -0.5),
        'o_w_out': nrm(16, (O, ODD_MIX, D), ODD_MIX ** -0.5),
        'diff_lq1': nrm(17, (O, DIFF_HD), 0.1),
        'diff_lk1': nrm(18, (O, DIFF_HD), 0.1),
        'diff_lq2': nrm(19, (O, DIFF_HD), 0.1),
        'diff_lk2': nrm(20, (O, DIFF_HD), 0.1),
        'diff_subln': 1.0 + nrm(21, (O, 2 * DIFF_HD), 0.01),
        'lru_conv_w': nrm(22, (O, CONV_W, LRU_W), CONV_W ** -0.5),
        'lru_conv_b': nrm(23, (O, LRU_W), 0.02),
        'lru_lambda': jnp.log(a) - jnp.log1p(-a),
        'lru_wa': nrm(25, (O, 2, LRU_BLOCKS, LRU_BW, LRU_BW), LRU_BW ** -0.5),
        'lru_ba': nrm(26, (O, 2, LRU_W), 0.02),
        'lru_wx': nrm(27, (O, 2, LRU_BLOCKS, LRU_BW, LRU_BW), LRU_BW ** -0.5),
        'lru_bx': nrm(28, (O, 2, LRU_W), 0.02),
    }


def reference(x, c, ctx, c_ctx, mod_w, mod_b, norm_g, final_g, e_w_in, e_w_out, na_rpb,
              mla_q_norm, mla_w_uq, mla_kv_norm, mla_w_ukv, o_w_in, o_w_out,
              diff_lq1, diff_lk1, diff_lq2, diff_lk2, diff_subln, lru_conv_w, lru_conv_b,
              lru_lambda, lru_wa, lru_ba, lru_wx, lru_bx):
    xl, xc = x, ctx
    s_lat = jax.nn.silu(c)
    s_ctx = jax.nn.silu(c_ctx)
    for l in range(DEPTH):
        ctx_out = l < DEPTH - 1
        i = l // 2
        sh_l, sc_l, g_l = jnp.split(s_lat @ mod_w[l] + mod_b[l], 3, axis=-1)
        sh_c, sc_c, g_c = jnp.split(s_ctx @ mod_w[l] + mod_b[l], 3, axis=-1)
        hl = _rmsnorm(xl, norm_g[l]) * (1.0 + sc_l[:, None]) + sh_l[:, None]
        hc = _rmsnorm(xc, norm_g[l]) * (1.0 + sc_c) + sh_c
        if l % 2 == 0:
            yl, yc = _even_mixer(hl, hc, e_w_in[i], e_w_out[i], na_rpb[i], mla_q_norm[i],
                                 mla_w_uq[i], mla_kv_norm[i], mla_w_ukv[i], ctx_out)
        else:
            lam_init = 0.8 - 0.6 * math.exp(-0.3 * l)
            yl, yc = _odd_mixer(hl, hc, o_w_in[i], o_w_out[i], diff_lq1[i], diff_lk1[i],
                                diff_lq2[i], diff_lk2[i], diff_subln[i], lru_conv_w[i],
                                lru_conv_b[i], lru_lambda[i], lru_wa[i], lru_ba[i],
                                lru_wx[i], lru_bx[i], lam_init, ctx_out)
        xl = xl + g_l[:, None] * yl
        if ctx_out:
            xc = xc + g_c * yc
    return _rmsnorm(xl, final_g)
```

```python
import functools
import math

import jax
import jax.numpy as jnp
from jax import lax
from jax.experimental import pallas as pl
from jax.experimental.pallas import tpu as pltpu

F32 = jnp.float32
BF16 = jnp.bfloat16

D_MODEL = 1024
DEPTH = 4
GRID_W = 64
EPS = 1e-6
ROPE_BASE = 10000.0

NA_HEADS = 8
NA_HD = 64
NA_W = NA_HEADS * NA_HD
WIN_R = 8
WIN_C = 16

MLA_HEADS = 8
MLA_NOPE = 64
MLA_ROPE = 32
MLA_V = 64
MLA_QK = MLA_NOPE + MLA_ROPE
MLA_W = MLA_HEADS * MLA_V
Q_LORA = 384
KV_LORA = 256

DIFF_HEADS = 4
DIFF_HD = 64
DIFF_W = DIFF_HEADS * 2 * DIFF_HD

LRU_W = 512
LRU_BLOCKS = 8
LRU_BW = LRU_W // LRU_BLOCKS
CONV_W = 4
LRU_C = 8.0

LANES = 128
SUBLANES = 8
ROW_BLOCK = 256
KV_CHUNK = 512
VMEM_LIMIT = 56 * 1024 * 1024
NEG_BIG = -1e30

_NT = (((1,), (1,)), ((), ()))


def _dot_nt(a, b):
    return lax.dot_general(a, b, _NT, preferred_element_type=F32)


def _dot(a, b):
    return jnp.dot(a, b, preferred_element_type=F32)


def _params(sem):
    return pltpu.CompilerParams(dimension_semantics=sem, vmem_limit_bytes=VMEM_LIMIT)


def _mod_kernel(c_ref, w_ref, b_ref, o_ref):
    c = c_ref[...]
    s = c * jax.nn.sigmoid(c)
    o_ref[...] = _dot(s.astype(BF16), w_ref[...].astype(BF16)) + b_ref[...]


def _modulation(cvec, mod_w, mod_b):
    depth, d, d3 = mod_w.shape
    tn = 512
    return pl.pallas_call(
        _mod_kernel,
        grid=(depth, d3 // tn),
        in_specs=[
            pl.BlockSpec((SUBLANES, d), lambda l, j: (0, 0)),
            pl.BlockSpec((None, d, tn), lambda l, j: (l, 0, j)),
            pl.BlockSpec((None, 1, tn), lambda l, j: (l, 0, j)),
        ],
        out_specs=pl.BlockSpec((None, SUBLANES, tn), lambda l, j: (l, 0, j)),
        out_shape=jax.ShapeDtypeStruct((depth, SUBLANES, d3), F32),
        compiler_params=_params(("parallel", "parallel")),
        name="modulation",
    )(cvec, mod_w, mod_b.reshape(depth, 1, d3))


def _norm_mod(x, g, mod):
    d = x.shape[-1]
    y = x * lax.rsqrt(jnp.mean(x * x, axis=-1, keepdims=True) + EPS) * g
    return y * (1.0 + mod[:, d:2 * d]) + mod[:, 0:d]


def _rms(x, g):
    return x * lax.rsqrt(jnp.mean(x * x, axis=-1, keepdims=True) + EPS) * g


E_Q, E_K, E_V, E_G, E_LQ, E_LKV, E_KR, E_KRS, E_END = 0, 512, 1024, 1536, 2560, 2944, 3200, 3328, 3456


def _in_even_kernel(x_ref, g_ref, mod_ref, w_ref, qn_ref, wuq_ref, kvn_ref, wkv_ref, c_ref, s_ref,
                    naq_ref, nak_ref, nav_ref, gate_ref, mq_ref, mk_ref, mv_ref):
    hb = _norm_mod(x_ref[...], g_ref[...], mod_ref[...]).astype(BF16)
    npair = NA_HEADS // 2
    for ref, off in ((naq_ref, E_Q), (nak_ref, E_K), (nav_ref, E_V)):
        seg = _dot(hb, w_ref[:, off:off + NA_W])
        for p in range(npair):
            ref[p] = seg[:, p * LANES:(p + 1) * LANES].astype(BF16)
    gate_ref[...] = _dot(hb, w_ref[:, E_G:E_LQ])
    cos = c_ref[...]
    sin = s_ref[...]
    qn = _rms(_dot(hb, w_ref[:, E_LQ:E_LKV]), qn_ref[...]).astype(BF16)
    width = MLA_HEADS * LANES
    q_main = _dot(qn, wuq_ref[:, 0:width])
    q_swap = _dot(qn, wuq_ref[:, width:2 * width])
    scale = MLA_QK ** -0.5
    for h in range(MLA_HEADS):
        sl = slice(h * LANES, (h + 1) * LANES)
        mq_ref[h] = ((q_main[:, sl] * cos + q_swap[:, sl] * sin) * scale).astype(BF16)
    kvn = _rms(_dot(hb, w_ref[:, E_LKV:E_KR]), kvn_ref[...]).astype(BF16)
    kr = _dot(hb, w_ref[:, E_KR:E_KRS]) * cos + _dot(hb, w_ref[:, E_KRS:E_END]) * sin
    k_nope = _dot(kvn, wkv_ref[:, 0:width])
    for h in range(MLA_HEADS):
        mk_ref[h] = (k_nope[:, h * LANES:(h + 1) * LANES] + kr).astype(BF16)
    v = _dot(kvn, wkv_ref[:, width:width + MLA_W])
    for p in range(MLA_HEADS // 2):
        mv_ref[p] = v[:, p * LANES:(p + 1) * LANES].astype(BF16)


def _in_even(x, norm_g, mod, w, qn, wuq, kvn, wkv, cos, sin):
    t, d = x.shape
    nb = t // ROW_BLOCK
    full = lambda a: pl.BlockSpec(a.shape, lambda i: (0,) * a.ndim)
    rows = lambda c: pl.BlockSpec((ROW_BLOCK, c), lambda i: (i, 0))
    heads = lambda n: pl.BlockSpec((n, ROW_BLOCK, LANES), lambda i: (0, i, 0))
    hshape = lambda n: jax.ShapeDtypeStruct((n, t, LANES), BF16)
    return pl.pallas_call(
        _in_even_kernel,
        grid=(nb,),
        in_specs=[rows(d), full(norm_g),
                  pl.BlockSpec((None, 1, 3 * d), lambda i: (jnp.minimum(i, 1), 0, 0)),
                  full(w), full(qn), full(wuq), full(kvn), full(wkv), rows(LANES), rows(LANES)],
        out_specs=[heads(4), heads(4), heads(4), rows(2 * NA_W), heads(8), heads(8), heads(4)],
        out_shape=[hshape(4), hshape(4), hshape(4), jax.ShapeDtypeStruct((t, 2 * NA_W), F32),
                   hshape(8), hshape(8), hshape(4)],
        compiler_params=_params(("parallel",)),
        name="in_proj_even",
    )(x, norm_g, mod, w, qn, wuq, kvn, wkv, cos, sin)


O_Q, O_QS, O_K, O_KS, O_V, O_G, O_U, O_END = 0, 512, 1024, 1536, 2048, 2560, 3584, 4096


def _in_odd_kernel(x_ref, g_ref, mod_ref, w_ref, c_ref, s_ref, dq_ref, dk_ref, dv_ref, gate_ref, u_ref):
    hb = _norm_mod(x_ref[...], g_ref[...], mod_ref[...]).astype(BF16)
    cos = c_ref[...]
    sin = s_ref[...]
    lane = lax.broadcasted_iota(jnp.int32, (1, LANES), 1)
    first = lane < DIFF_HD
    q = _dot(hb, w_ref[:, O_Q:O_QS])
    qs = _dot(hb, w_ref[:, O_QS:O_K])
    k = _dot(hb, w_ref[:, O_K:O_KS])
    ks = _dot(hb, w_ref[:, O_KS:O_V])
    v = _dot(hb, w_ref[:, O_V:O_G])
    for h in range(DIFF_HEADS):
        sl = slice(h * LANES, (h + 1) * LANES)
        qr = q[:, sl] * cos + qs[:, sl] * sin
        dq_ref[h, 0] = jnp.where(first, qr, 0.0).astype(BF16)
        dq_ref[h, 1] = jnp.where(first, 0.0, qr).astype(BF16)
        dk_ref[h] = (k[:, sl] * cos + ks[:, sl] * sin).astype(BF16)
        dv_ref[h] = v[:, sl].astype(BF16)
    gate_ref[...] = _dot(hb, w_ref[:, O_G:O_U])
    u_ref[...] = _dot(hb, w_ref[:, O_U:O_END])


def _in_odd(x, norm_g, mod, w, cos, sin):
    t, d = x.shape
    nb = t // ROW_BLOCK
    full = lambda a: pl.BlockSpec(a.shape, lambda i: (0,) * a.ndim)
    rows = lambda c: pl.BlockSpec((ROW_BLOCK, c), lambda i: (i, 0))
    heads = pl.BlockSpec((DIFF_HEADS, ROW_BLOCK, LANES), lambda i: (0, i, 0))
    hshape = jax.ShapeDtypeStruct((DIFF_HEADS, t, LANES), BF16)
    return pl.pallas_call(
        _in_odd_kernel,
        grid=(nb,),
        in_specs=[rows(d), full(norm_g),
                  pl.BlockSpec((None, 1, 3 * d), lambda i: (jnp.minimum(i, 1), 0, 0)),
                  full(w), rows(LANES), rows(LANES)],
        out_specs=[pl.BlockSpec((DIFF_HEADS, 2, ROW_BLOCK, LANES), lambda i: (0, 0, i, 0)),
                   heads, heads, rows(DIFF_W + LRU_W), rows(LRU_W)],
        out_shape=[jax.ShapeDtypeStruct((DIFF_HEADS, 2, t, LANES), BF16), hshape, hshape,
                   jax.ShapeDtypeStruct((t, DIFF_W + LRU_W), F32), jax.ShapeDtypeStruct((t, LRU_W), F32)],
        compiler_params=_params(("parallel",)),
        name="in_proj_odd",
    )(x, norm_g, mod, w, cos, sin)


def _flash_kernel(*refs, diff, lam_init, n_lat_chunks):
    if diff:
        q_ref, k_ref, v_ref, lamv_ref, subln_ref, o_ref, m_sc, l_sc, acc_sc = refs
    else:
        q_ref, k_ref, v_ref, o_ref, m_sc, l_sc, acc_sc = refs
    i = pl.program_id(1)

    def keys(s, start, size):
        if diff:
            return k_ref[pl.ds(start, size), :]
        return k_ref[s, pl.ds(start, size), :]

    vc = v_ref[0:ROW_BLOCK, :]
    for s in range(2):
        sc = _dot_nt(q_ref[s], keys(s, 0, ROW_BLOCK))
        m = jnp.max(sc, axis=-1, keepdims=True)
        p = jnp.exp(sc - m)
        m_sc[s] = m
        l_sc[s] = jnp.sum(p, axis=-1, keepdims=True)
        acc_sc[s] = _dot(p.astype(BF16), vc)

    def body(j, carry):
        start = pl.multiple_of(ROW_BLOCK + j * KV_CHUNK, ROW_BLOCK)
        vj = v_ref[pl.ds(start, KV_CHUNK), :]
        for s in range(2):
            sc = _dot_nt(q_ref[s], keys(s, start, KV_CHUNK))
            m_prev = m_sc[s]
            m_new = jnp.maximum(m_prev, jnp.max(sc, axis=-1, keepdims=True))
            alpha = jnp.exp(m_prev - m_new)
            p = jnp.exp(sc - m_new)
            l_sc[s] = alpha * l_sc[s] + jnp.sum(p, axis=-1, keepdims=True)
            acc_sc[s] = alpha * acc_sc[s] + _dot(p.astype(BF16), vj)
            m_sc[s] = m_new
        return carry

    lax.fori_loop(0, jnp.where(i == 0, 0, n_lat_chunks), body, 0)

    o0 = acc_sc[0] / l_sc[0]
    o1 = acc_sc[1] / l_sc[1]
    if diff:
        lv = lamv_ref[...]
        lam = (jnp.exp(jnp.sum(lv[0:1] * lv[1:2], axis=-1, keepdims=True))
               - jnp.exp(jnp.sum(lv[2:3] * lv[3:4], axis=-1, keepdims=True)) + lam_init)
        o = o0 - lam * o1
        o_ref[...] = _rms(o, subln_ref[...]) * (1.0 - lam_init)
    else:
        lane = lax.broadcasted_iota(jnp.int32, (1, LANES), 1)
        o_ref[...] = jnp.where(lane < MLA_V, o0, o1)


def _flash(q, k, v, extra, *, diff, lam_init=0.0):
    groups, t = v.shape[0], v.shape[1]
    nb = t // ROW_BLOCK
    n_lat_chunks = (t - ROW_BLOCK) // KV_CHUNK
    if diff:
        q_spec = pl.BlockSpec((None, 2, ROW_BLOCK, LANES), lambda g, i: (g, 0, i, 0))
        k_spec = pl.BlockSpec((None, t, LANES), lambda g, i: (g, 0, 0))
        extra_specs = [pl.BlockSpec(e.shape, lambda g, i: (0, 0)) for e in extra]
    else:
        q_spec = pl.BlockSpec((2, ROW_BLOCK, LANES), lambda g, i: (g, i, 0))
        k_spec = pl.BlockSpec((2, t, LANES), lambda g, i: (g, 0, 0))
        extra_specs = []
    return pl.pallas_call(
        functools.partial(_flash_kernel, diff=diff, lam_init=lam_init, n_lat_chunks=n_lat_chunks),
        grid=(groups, nb),
        in_specs=[q_spec, k_spec, pl.BlockSpec((None, t, LANES), lambda g, i: (g, 0, 0))] + extra_specs,
        out_specs=pl.BlockSpec((ROW_BLOCK, LANES), lambda g, i: (i, g)),
        out_shape=jax.ShapeDtypeStruct((t, groups * LANES), F32),
        scratch_shapes=[pltpu.VMEM((2, ROW_BLOCK, 1), F32), pltpu.VMEM((2, ROW_BLOCK, 1), F32),
                        pltpu.VMEM((2, ROW_BLOCK, LANES), F32)],
        compiler_params=_params(("parallel", "arbitrary")),
        name="flash_diff" if diff else "flash_mla",
    )(q, k, v, *extra)


NA_ROWS_PER_STEP = ROW_BLOCK // GRID_W


def _na_kernel(q_ref, k_ref, v_ref, bt_ref, o_ref, *, grid_rows):
    i = pl.program_id(1)
    lane = lax.broadcasted_iota(jnp.int32, (1, LANES), 1)
    first = lane < NA_HD
    kctx = k_ref[0:ROW_BLOCK, :]
    vctx = v_ref[0:ROW_BLOCK, :]

    def stacked(q):
        zero = jnp.zeros_like(q)
        return jnp.concatenate([jnp.where(first, q, zero), jnp.where(first, zero, q)], axis=0)

    @pl.when(i == 0)
    def _():
        q2 = stacked(q_ref[...])
        sc = _dot_nt(q2, kctx)
        p = jnp.exp(sc - jnp.max(sc, axis=-1, keepdims=True))
        o = _dot(p.astype(BF16), vctx) / jnp.sum(p, axis=-1, keepdims=True)
        o_ref[...] = jnp.where(first, o[0:ROW_BLOCK], o[ROW_BLOCK:])

    @pl.when(i > 0)
    def _():
        nkeys = WIN_R * GRID_W
        for rr in range(NA_ROWS_PER_STEP):
            r = (i - 1) * NA_ROWS_PER_STEP + rr
            r0 = jnp.clip(r - WIN_R // 2, 0, grid_rows - WIN_R)
            start = pl.multiple_of(ROW_BLOCK + r0 * GRID_W, GRID_W)
            kl = k_ref[pl.ds(start, nkeys), :]
            vl = v_ref[pl.ds(start, nkeys), :]
            q2 = stacked(q_ref[rr * GRID_W:(rr + 1) * GRID_W, :])
            s_lat = _dot_nt(q2, kl) + bt_ref[r0 - r + WIN_R - 1]
            s_ctx = _dot_nt(q2, kctx)
            m = jnp.maximum(jnp.max(s_lat, axis=-1, keepdims=True), jnp.max(s_ctx, axis=-1, keepdims=True))
            p_lat = jnp.exp(s_lat - m)
            p_ctx = jnp.exp(s_ctx - m)
            l = jnp.sum(p_lat, axis=-1, keepdims=True) + jnp.sum(p_ctx, axis=-1, keepdims=True)
            o = (_dot(p_lat.astype(BF16), vl) + _dot(p_ctx.astype(BF16), vctx)) / l
            o_ref[rr * GRID_W:(rr + 1) * GRID_W, :] = jnp.where(first, o[0:GRID_W], o[GRID_W:])


def _na_attention(q, k, v, bias):
    npair, t = q.shape[0], q.shape[1]
    nb = t // ROW_BLOCK
    grid_rows = (t - ROW_BLOCK) // GRID_W
    resident = pl.BlockSpec((None, t, LANES), lambda p, i: (p, 0, 0))
    return pl.pallas_call(
        functools.partial(_na_kernel, grid_rows=grid_rows),
        grid=(npair, nb),
        in_specs=[pl.BlockSpec((None, ROW_BLOCK, LANES), lambda p, i: (p, i, 0)), resident, resident,
                  pl.BlockSpec((None,) + bias.shape[1:], lambda p, i: (p, 0, 0, 0))],
        out_specs=pl.BlockSpec((ROW_BLOCK, LANES), lambda p, i: (i, p)),
        out_shape=jax.ShapeDtypeStruct((t, npair * LANES), F32),
        compiler_params=_params(("parallel", "arbitrary")),
        name="na_attention",
    )(q, k, v, bias)


def _na_bias_table(rpb):
    w = jnp.arange(GRID_W)
    c0 = jnp.clip(w - WIN_C // 2, 0, GRID_W - WIN_C)
    j = jnp.arange(GRID_W)
    inside = (j[None, :] >= c0[:, None]) & (j[None, :] < c0[:, None] + WIN_C)
    col_off = jnp.clip(j[None, :] - w[:, None] + WIN_C - 1, 0, 2 * WIN_C - 2)
    row_off = jnp.arange(WIN_R)[:, None] + jnp.arange(WIN_R)[None, :]
    tab = rpb[:, row_off][:, :, :, col_off]
    tab = jnp.where(inside[None, None, None], tab, NEG_BIG)
    tab = tab.transpose(0, 1, 3, 2, 4).reshape(NA_HEADS // 2, 2, WIN_R, GRID_W, WIN_R * GRID_W)
    return tab.transpose(0, 2, 1, 3, 4).reshape(NA_HEADS // 2, WIN_R, 2 * GRID_W, WIN_R * GRID_W)


def _out_kernel(a_ref, b_ref, gate_ref, w_ref, x_ref, mod_ref, o_ref):
    half = a_ref.shape[-1]
    d = x_ref.shape[-1]
    ga = gate_ref[:, 0:half]
    gb = gate_ref[:, half:2 * half]
    ya = (a_ref[...] * (ga * jax.nn.sigmoid(ga))).astype(BF16)
    yb = (b_ref[...] * (gb * jax.nn.sigmoid(gb))).astype(BF16)
    y = _dot(ya, w_ref[0:half, :]) + _dot(yb, w_ref[half:2 * half, :])
    o_ref[...] = x_ref[...] + mod_ref[:, 2 * d:3 * d] * y


def _out_proj(a, b, gate, w, x, mod):
    t, d = x.shape
    nb = t // ROW_BLOCK
    rows = lambda c: pl.BlockSpec((ROW_BLOCK, c), lambda i: (i, 0))
    return pl.pallas_call(
        _out_kernel,
        grid=(nb,),
        in_specs=[rows(a.shape[1]), rows(b.shape[1]), rows(gate.shape[1]),
                  pl.BlockSpec(w.shape, lambda i: (0, 0)), rows(d),
                  pl.BlockSpec((None, 1, 3 * d), lambda i: (jnp.minimum(i, 1), 0, 0))],
        out_specs=rows(d),
        out_shape=jax.ShapeDtypeStruct((t, d), F32),
        compiler_params=_params(("parallel",)),
        name="out_proj",
    )(a, b, gate, w, x, mod)


def _lru_kernel(*refs, reverse, nb):
    if reverse:
        (x_ref, xp_ref, xn_ref, cw_ref, cb_ref, lam_ref, wa_ref, ba_ref, wx_ref, bx_ref, hf_ref,
         o_ref, xs_sc, a_sc, b_sc, carry_sc) = refs
    else:
        (x_ref, xp_ref, xn_ref, cw_ref, cb_ref, lam_ref, wa_ref, ba_ref, wx_ref, bx_ref,
         o_ref, xs_sc, a_sc, b_sc, carry_sc) = refs
    step = pl.program_id(0)
    blk = jnp.where(step == 0, 0, nb - step) if reverse else step
    tb = x_ref.shape[0]

    @pl.when(step == 0)
    def _():
        carry_sc[...] = jnp.zeros_like(carry_sc)

    prev_ok = blk >= 2
    next_ok = (blk >= 1) & (blk < nb - 1)
    xs_sc[0:SUBLANES, :] = jnp.where(prev_ok, xp_ref[...], 0.0)
    xs_sc[SUBLANES:SUBLANES + tb, :] = x_ref[...]
    xs_sc[SUBLANES + tb:2 * SUBLANES + tb, :] = jnp.where(next_ok, xn_ref[...], 0.0)
    u = cb_ref[...]
    for j in range(CONV_W):
        off = SUBLANES + j - CONV_W // 2
        u = u + xs_sc[off:off + tb, :] * cw_ref[j:j + 1, :]

    ub = u.astype(BF16)
    r = jax.nn.sigmoid(_dot(ub, wa_ref[...]) + ba_ref[...])
    gi = jax.nn.sigmoid(_dot(ub, wx_ref[...]) + bx_ref[...])
    nl = -lam_ref[...]
    softplus = jnp.maximum(nl, 0.0) + jnp.log1p(jnp.exp(-jnp.abs(nl)))
    log_a = -LRU_C * r * softplus
    a_sc[...] = jnp.exp(log_a)
    b_sc[...] = jnp.sqrt(1.0 - jnp.exp(2.0 * log_a)) * (gi * u)

    row = lax.broadcasted_iota(jnp.int32, (SUBLANES, 1), 0)
    nchunk = tb // SUBLANES

    def chunk(c, carry):
        cc = nchunk - 1 - c if reverse else c
        start = pl.multiple_of(cc * SUBLANES, SUBLANES)
        a = a_sc[pl.ds(start, SUBLANES), :]
        b = b_sc[pl.ds(start, SUBLANES), :]
        for k in (1, 2, 4):
            shift = SUBLANES - k if reverse else k
            valid = (row < SUBLANES - k) if reverse else (row >= k)
            a_sh = pltpu.roll(a, shift, 0)
            b_sh = pltpu.roll(b, shift, 0)
            b = jnp.where(valid, a * b_sh + b, b)
            a = jnp.where(valid, a * a_sh, a)
        h = b + a * carry
        if reverse:
            o_ref[pl.ds(start, SUBLANES), :] = h + hf_ref[pl.ds(start, SUBLANES), :]
            new = h[0:1, :]
        else:
            o_ref[pl.ds(start, SUBLANES), :] = h
            new = h[SUBLANES - 1:SUBLANES, :]
        return jnp.broadcast_to(new, carry.shape)

    carry_sc[...] = lax.fori_loop(0, nchunk, chunk, carry_sc[...])


def _lru_pass(x, conv_w, conv_b, lam, wa, ba, wx, bx, hf, *, reverse):
    t, c = x.shape
    tb = ROW_BLOCK
    nb = t // tb
    per8 = tb // SUBLANES
    nb8 = t // SUBLANES
    if reverse:
        blk = lambda s: jnp.where(s == 0, 0, nb - s)
    else:
        blk = lambda s: s
    full = lambda a: pl.BlockSpec(a.shape, lambda s: (0,) * a.ndim)
    main = pl.BlockSpec((tb, c), lambda s: (blk(s), 0))
    args = [x, x, x, conv_w, conv_b, lam, wa, ba, wx, bx]
    in_specs = [main,
                pl.BlockSpec((SUBLANES, c), lambda s: (jnp.maximum(blk(s) * per8 - 1, 0), 0)),
                pl.BlockSpec((SUBLANES, c), lambda s: (jnp.minimum((blk(s) + 1) * per8, nb8 - 1), 0)),
                full(conv_w), full(conv_b), full(lam), full(wa), full(ba), full(wx), full(bx)]
    if reverse:
        args.append(hf)
        in_specs.append(main)
    return pl.pallas_call(
        functools.partial(_lru_kernel, reverse=reverse, nb=nb),
        grid=(nb,),
        in_specs=in_specs,
        out_specs=main,
        out_shape=jax.ShapeDtypeStruct((t, c), F32),
        scratch_shapes=[pltpu.VMEM((tb + 2 * SUBLANES, c), F32), pltpu.VMEM((tb, c), F32),
                        pltpu.VMEM((tb, c), F32), pltpu.VMEM((SUBLANES, c), F32)],
        compiler_params=_params(("arbitrary",)),
        name="lru_bwd" if reverse else "lru_fwd",
    )(*args)


def _block_diag(w):
    g, bw, _ = w.shape
    eye = jnp.eye(g, dtype=w.dtype)
    return (eye[:, None, :, None] * w[:, :, None, :]).reshape(g * bw, g * bw)


def _final_kernel(x_ref, g_ref, o_ref):
    o_ref[...] = _rms(x_ref[...], g_ref[...])


def _final_norm(x, g, n):
    t, d = x.shape
    skip = (t - n) // ROW_BLOCK
    return pl.pallas_call(
        _final_kernel,
        grid=(n // ROW_BLOCK,),
        in_specs=[pl.BlockSpec((ROW_BLOCK, d), lambda i: (i + skip, 0)), pl.BlockSpec((1, d), lambda i: (0, 0))],
        out_specs=pl.BlockSpec((ROW_BLOCK, d), lambda i: (i, 0)),
        out_shape=jax.ShapeDtypeStruct((n, d), F32),
        compiler_params=_params(("parallel",)),
        name="final_norm",
    )(x, g)


def _rope_tables(n, ctx_len, rot_dim, lead, tail, repeat):
    t = jnp.arange(n)
    row = (t // GRID_W).astype(F32)
    col = (t % GRID_W).astype(F32)
    n_freq = rot_dim // 4
    inv = ROPE_BASE ** (-jnp.arange(n_freq, dtype=F32) / n_freq)
    ang = jnp.concatenate([row[:, None] * inv, col[:, None] * inv], axis=-1)
    cos, sin = jnp.cos(ang), jnp.sin(ang)
    cos_l = jnp.concatenate([jnp.ones((n, lead), F32)] + [cos, cos] * repeat + [jnp.ones((n, tail), F32)], axis=-1)
    sin_l = jnp.concatenate([jnp.zeros((n, lead), F32)] + [-sin, sin] * repeat + [jnp.zeros((n, tail), F32)], axis=-1)
    cos_t = jnp.concatenate([jnp.ones((ctx_len, LANES), F32), cos_l], axis=0)
    sin_t = jnp.concatenate([jnp.zeros((ctx_len, LANES), F32), sin_l], axis=0)
    return cos_t, sin_t


def _even_weights(w_in, w_uq, w_ukv):
    d = w_in.shape[0]
    q, k, v, g_na, lq, lkv, kr, g_mla = jnp.split(
        w_in, [512, 1024, 1536, 2048, 2048 + Q_LORA, 2048 + Q_LORA + KV_LORA, 2048 + Q_LORA + KV_LORA + MLA_ROPE], axis=1)
    half = MLA_ROPE // 2
    pad = LANES - MLA_QK
    kr_pad = jnp.concatenate([jnp.zeros((d, MLA_NOPE), F32), kr, jnp.zeros((d, pad), F32)], axis=1)
    krs_pad = jnp.concatenate([jnp.zeros((d, MLA_NOPE), F32), kr[:, half:], kr[:, :half], jnp.zeros((d, pad), F32)], axis=1)
    w = jnp.concatenate([q * (NA_HD ** -0.5), k, v, g_na, g_mla, lq, lkv, kr_pad, krs_pad], axis=1).astype(BF16)
    uq = w_uq.reshape(Q_LORA, MLA_HEADS, MLA_QK)
    zq = jnp.zeros((Q_LORA, MLA_HEADS, pad), F32)
    uq_main = jnp.concatenate([uq, zq], axis=-1)
    uq_swap = jnp.concatenate([jnp.zeros((Q_LORA, MLA_HEADS, MLA_NOPE), F32), uq[..., MLA_NOPE + half:],
                               uq[..., MLA_NOPE:MLA_NOPE + half], zq], axis=-1)
    wuq = jnp.concatenate([uq_main.reshape(Q_LORA, -1), uq_swap.reshape(Q_LORA, -1)], axis=1).astype(BF16)
    ukv = w_ukv.reshape(KV_LORA, MLA_HEADS, MLA_NOPE + MLA_V)
    k_nope = jnp.concatenate([ukv[..., :MLA_NOPE], jnp.zeros((KV_LORA, MLA_HEADS, LANES - MLA_NOPE), F32)], axis=-1)
    wkv = jnp.concatenate([k_nope.reshape(KV_LORA, -1), ukv[..., MLA_NOPE:].reshape(KV_LORA, -1)], axis=1).astype(BF16)
    return w, wuq, wkv


def _odd_weights(w_in):
    d = w_in.shape[0]
    q, k, v, g_d, u, g_lru = jnp.split(w_in, [512, 1024, 1536, 2048, 2560], axis=1)

    def swap(m):
        return m.reshape(d, DIFF_HEADS, 2, 2, DIFF_HD // 2)[:, :, :, ::-1, :].reshape(d, DIFF_W)

    scale = DIFF_HD ** -0.5
    return jnp.concatenate([q * scale, swap(q) * scale, k, swap(k), v, g_d, g_lru, u], axis=1).astype(BF16)


def kernel(x, c, ctx, c_ctx, mod_w, mod_b, norm_g, final_g, e_w_in, e_w_out, na_rpb, mla_q_norm, mla_w_uq, mla_kv_norm, mla_w_ukv, o_w_in, o_w_out, diff_lq1, diff_lk1, diff_lq2, diff_lk2, diff_subln, lru_conv_w, lru_conv_b, lru_lambda, lru_wa, lru_ba, lru_wx, lru_bx):
    batch, n, d = x.shape
    ctx_len = ctx.shape[1]
    assert batch == 1 and ctx_len == ROW_BLOCK and d == D_MODEL
    assert n % KV_CHUNK == 0 and n // GRID_W >= WIN_R

    cvec = jnp.concatenate([c_ctx[None], c, jnp.zeros((SUBLANES - 2, d), F32)], axis=0)
    mods = _modulation(cvec, mod_w, mod_b)[:, 0:2].reshape(DEPTH, 2, 1, 3 * d)

    cos_m, sin_m = _rope_tables(n, ctx_len, MLA_ROPE, MLA_NOPE, LANES - MLA_QK, 1)
    cos_d, sin_d = _rope_tables(n, ctx_len, DIFF_HD, 0, 0, 2)

    xs = jnp.concatenate([ctx[0], x[0]], axis=0)
    for l in range(DEPTH):
        i = l // 2
        g_row = norm_g[l][None]
        if l % 2 == 0:
            w, wuq, wkv = _even_weights(e_w_in[i], mla_w_uq[i], mla_w_ukv[i])
            naq, nak, nav, gate, mq, mk, mv = _in_even(
                xs, g_row, mods[l], w, mla_q_norm[i][None], wuq, mla_kv_norm[i][None], wkv, cos_m, sin_m)
            a = _na_attention(naq, nak, nav, _na_bias_table(na_rpb[i]))
            b = _flash(mq, mk, mv, (), diff=False)
            w_out = e_w_out[i]
        else:
            lam_init = 0.8 - 0.6 * math.exp(-0.3 * l)
            dq, dk, dv, gate, u = _in_odd(xs, g_row, mods[l], _odd_weights(o_w_in[i]), cos_d, sin_d)
            lamv = jnp.stack([diff_lq1[i], diff_lk1[i], diff_lq2[i], diff_lk2[i]])
            a = _flash(dq, dk, dv, (lamv, diff_subln[i][None]), diff=True, lam_init=lam_init)
            lru = lambda dr, hf: _lru_pass(
                u, lru_conv_w[i], lru_conv_b[i][None], lru_lambda[i, dr][None],
                _block_diag(lru_wa[i, dr]).astype(BF16), lru_ba[i, dr][None],
                _block_diag(lru_wx[i, dr]).astype(BF16), lru_bx[i, dr][None], hf, reverse=dr == 1)
            b = lru(1, lru(0, None))
            w_out = o_w_out[i]
        xs = _out_proj(a, b, gate, w_out.astype(BF16), xs, mods[l])
    return _final_norm(xs, final_g[None], n)[None]
```

```python
import functools
import math

import jax
import jax.numpy as jnp
from jax import lax
from jax.experimental import pallas as pl
from jax.experimental.pallas import tpu as pltpu

F32 = jnp.float32
BF16 = jnp.bfloat16

D_MODEL = 1024
DEPTH = 4
GRID_W = 64
EPS = 1e-6
ROPE_BASE = 10000.0

NA_HEADS = 8
NA_HD = 64
NA_W = NA_HEADS * NA_HD
WIN_R = 8
WIN_C = 16

MLA_HEADS = 8
MLA_NOPE = 64
MLA_ROPE = 32
MLA_V = 64
MLA_QK = MLA_NOPE + MLA_ROPE
MLA_W = MLA_HEADS * MLA_V
Q_LORA = 384
KV_LORA = 256

DIFF_HEADS = 4
DIFF_HD = 64
DIFF_W = DIFF_HEADS * 2 * DIFF_HD

LRU_W = 512
LRU_BLOCKS = 8
LRU_BW = LRU_W // LRU_BLOCKS
CONV_W = 4
LRU_C = 8.0

LANES = 128
SUBLANES = 8
ROW_BLOCK = 256
KV_CHUNK = 512
VMEM_LIMIT = 56 * 1024 * 1024
NEG_BIG = -1e30
LOG2E = math.log2(math.e)

_NT = (((1,), (1,)), ((), ()))


def _dot_nt(a, b):
    return lax.dot_general(a, b, _NT, preferred_element_type=F32)


def _dot(a, b):
    return jnp.dot(a, b, preferred_element_type=F32)


def _params(sem):
    return pltpu.CompilerParams(dimension_semantics=sem, vmem_limit_bytes=VMEM_LIMIT)


def _mod_kernel(c_ref, w_ref, b_ref, o_ref):
    c = c_ref[...]
    s = c * jax.nn.sigmoid(c)
    o_ref[...] = _dot(s.astype(BF16), w_ref[...].astype(BF16)) + b_ref[...]


def _modulation(cvec, mod_w, mod_b):
    depth, d, d3 = mod_w.shape
    tn = 512
    return pl.pallas_call(
        _mod_kernel,
        grid=(depth, d3 // tn),
        in_specs=[
            pl.BlockSpec((SUBLANES, d), lambda l, j: (0, 0)),
            pl.BlockSpec((None, d, tn), lambda l, j: (l, 0, j)),
            pl.BlockSpec((None, 1, tn), lambda l, j: (l, 0, j)),
        ],
        out_specs=pl.BlockSpec((None, SUBLANES, tn), lambda l, j: (l, 0, j)),
        out_shape=jax.ShapeDtypeStruct((depth, SUBLANES, d3), F32),
        compiler_params=_params(("parallel", "parallel")),
        name="modulation",
    )(cvec, mod_w, mod_b.reshape(depth, 1, d3))


def _norm_mod(x, g, mod):
    d = x.shape[-1]
    y = x * lax.rsqrt(jnp.mean(x * x, axis=-1, keepdims=True) + EPS) * g
    return y * (1.0 + mod[:, d:2 * d]) + mod[:, 0:d]


def _rms(x, g):
    return x * lax.rsqrt(jnp.mean(x * x, axis=-1, keepdims=True) + EPS) * g


E_Q, E_K, E_V, E_G, E_LQ, E_LKV, E_KR, E_KRS, E_END = 0, 512, 1024, 1536, 2560, 2944, 3200, 3328, 3456


def _in_even_kernel(x_ref, g_ref, mod_ref, w_ref, qn_ref, wuq_ref, kvn_ref, wkv_ref, c_ref, s_ref,
                    naq_ref, nak_ref, nav_ref, gate_ref, mq_ref, mk_ref, mv_ref):
    hb = _norm_mod(x_ref[...], g_ref[...], mod_ref[...]).astype(BF16)
    npair = NA_HEADS // 2
    for ref, off in ((naq_ref, E_Q), (nak_ref, E_K), (nav_ref, E_V)):
        seg = _dot(hb, w_ref[:, off:off + NA_W])
        for p in range(npair):
            ref[p] = seg[:, p * LANES:(p + 1) * LANES].astype(BF16)
    gate_ref[...] = _dot(hb, w_ref[:, E_G:E_LQ])
    cos = c_ref[...]
    sin = s_ref[...]
    qn = _rms(_dot(hb, w_ref[:, E_LQ:E_LKV]), qn_ref[...]).astype(BF16)
    width = MLA_HEADS * LANES
    q_main = _dot(qn, wuq_ref[:, 0:width])
    q_swap = _dot(qn, wuq_ref[:, width:2 * width])
    scale = MLA_QK ** -0.5 * LOG2E
    for h in range(MLA_HEADS):
        sl = slice(h * LANES, (h + 1) * LANES)
        mq_ref[h] = ((q_main[:, sl] * cos + q_swap[:, sl] * sin) * scale).astype(BF16)
    kvn = _rms(_dot(hb, w_ref[:, E_LKV:E_KR]), kvn_ref[...]).astype(BF16)
    kr = _dot(hb, w_ref[:, E_KR:E_KRS]) * cos + _dot(hb, w_ref[:, E_KRS:E_END]) * sin
    k_nope = _dot(kvn, wkv_ref[:, 0:width])
    for h in range(MLA_HEADS):
        mk_ref[h] = (k_nope[:, h * LANES:(h + 1) * LANES] + kr).astype(BF16)
    v = _dot(kvn, wkv_ref[:, width:width + MLA_W])
    for p in range(MLA_HEADS // 2):
        mv_ref[p] = v[:, p * LANES:(p + 1) * LANES].T.astype(BF16)


def _in_even(x, norm_g, mod, w, qn, wuq, kvn, wkv, cos, sin):
    t, d = x.shape
    nb = t // ROW_BLOCK
    full = lambda a: pl.BlockSpec(a.shape, lambda i: (0,) * a.ndim)
    rows = lambda c: pl.BlockSpec((ROW_BLOCK, c), lambda i: (i, 0))
    heads = lambda n: pl.BlockSpec((n, ROW_BLOCK, LANES), lambda i: (0, i, 0))
    hshape = lambda n: jax.ShapeDtypeStruct((n, t, LANES), BF16)
    return pl.pallas_call(
        _in_even_kernel,
        grid=(nb,),
        in_specs=[rows(d), full(norm_g),
                  pl.BlockSpec((None, 1, 3 * d), lambda i: (jnp.minimum(i, 1), 0, 0)),
                  full(w), full(qn), full(wuq), full(kvn), full(wkv), rows(LANES), rows(LANES)],
        out_specs=[heads(4), heads(4), heads(4), rows(2 * NA_W), heads(8), heads(8),
                   pl.BlockSpec((4, LANES, ROW_BLOCK), lambda i: (0, 0, i))],
        out_shape=[hshape(4), hshape(4), hshape(4), jax.ShapeDtypeStruct((t, 2 * NA_W), F32),
                   hshape(8), hshape(8), jax.ShapeDtypeStruct((4, LANES, t), BF16)],
        compiler_params=_params(("parallel",)),
        name="in_proj_even",
    )(x, norm_g, mod, w, qn, wuq, kvn, wkv, cos, sin)


O_Q, O_QS, O_K, O_KS, O_V, O_G, O_U, O_END = 0, 512, 1024, 1536, 2048, 2560, 3584, 4096


def _in_odd_kernel(x_ref, g_ref, mod_ref, w_ref, c_ref, s_ref, dq_ref, dk_ref, dv_ref, gate_ref, u_ref):
    hb = _norm_mod(x_ref[...], g_ref[...], mod_ref[...]).astype(BF16)
    cos = c_ref[...]
    sin = s_ref[...]
    lane = lax.broadcasted_iota(jnp.int32, (1, LANES), 1)
    first = lane < DIFF_HD
    q = _dot(hb, w_ref[:, O_Q:O_QS])
    qs = _dot(hb, w_ref[:, O_QS:O_K])
    k = _dot(hb, w_ref[:, O_K:O_KS])
    ks = _dot(hb, w_ref[:, O_KS:O_V])
    v = _dot(hb, w_ref[:, O_V:O_G])
    for h in range(DIFF_HEADS):
        sl = slice(h * LANES, (h + 1) * LANES)
        qr = (q[:, sl] * cos + qs[:, sl] * sin) * LOG2E
        dq_ref[h, 0] = jnp.where(first, qr, 0.0).astype(BF16)
        dq_ref[h, 1] = jnp.where(first, 0.0, qr).astype(BF16)
        dk_ref[h] = (k[:, sl] * cos + ks[:, sl] * sin).astype(BF16)
        dv_ref[h] = v[:, sl].T.astype(BF16)
    gate_ref[...] = _dot(hb, w_ref[:, O_G:O_U])
    u_ref[...] = _dot(hb, w_ref[:, O_U:O_END])


def _in_odd(x, norm_g, mod, w, cos, sin):
    t, d = x.shape
    nb = t // ROW_BLOCK
    full = lambda a: pl.BlockSpec(a.shape, lambda i: (0,) * a.ndim)
    rows = lambda c: pl.BlockSpec((ROW_BLOCK, c), lambda i: (i, 0))
    heads = pl.BlockSpec((DIFF_HEADS, ROW_BLOCK, LANES), lambda i: (0, i, 0))
    hshape = jax.ShapeDtypeStruct((DIFF_HEADS, t, LANES), BF16)
    return pl.pallas_call(
        _in_odd_kernel,
        grid=(nb,),
        in_specs=[rows(d), full(norm_g),
                  pl.BlockSpec((None, 1, 3 * d), lambda i: (jnp.minimum(i, 1), 0, 0)),
                  full(w), rows(LANES), rows(LANES)],
        out_specs=[pl.BlockSpec((DIFF_HEADS, 2, ROW_BLOCK, LANES), lambda i: (0, 0, i, 0)),
                   heads, pl.BlockSpec((DIFF_HEADS, LANES, ROW_BLOCK), lambda i: (0, 0, i)),
                   rows(DIFF_W + LRU_W), rows(LRU_W)],
        out_shape=[jax.ShapeDtypeStruct((DIFF_HEADS, 2, t, LANES), BF16), hshape,
                   jax.ShapeDtypeStruct((DIFF_HEADS, LANES, t), BF16),
                   jax.ShapeDtypeStruct((t, DIFF_W + LRU_W), F32), jax.ShapeDtypeStruct((t, LRU_W), F32)],
        compiler_params=_params(("parallel",)),
        name="in_proj_odd",
    )(x, norm_g, mod, w, cos, sin)


def _sublane_all(op, x):
    for k in (4, 2, 1):
        x = op(x, pltpu.roll(x, k, 0))
    return x


def _flash_kernel(*refs, diff, lam_init, n_lat_chunks):
    if diff:
        q_ref, k_ref, vt_ref, lamv_ref, subln_ref, o_ref, m_sc, l_sc, acc_sc, sa_sc, sb_sc = refs
    else:
        q_ref, k_ref, vt_ref, o_ref, m_sc, l_sc, acc_sc, sa_sc, sb_sc = refs
    i = pl.program_id(1)
    tq = q_ref.shape[1]
    dv = vt_ref.shape[0]
    t = vt_ref.shape[1]

    def scores(dst_sc, start, size):
        for s in range(2):
            kc = k_ref[pl.ds(start, size), :] if diff else k_ref[s, pl.ds(start, size), :]
            dst_sc[s, 0:size, :] = _dot_nt(kc, q_ref[s])

    def consume(src_sc, start, size):
        vt = vt_ref[:, pl.ds(start, size)]
        for s in range(2):
            st = src_sc[s, 0:size, :].reshape(size // SUBLANES, SUBLANES, tq)
            m_prev = m_sc[s]
            m_new = jnp.maximum(m_prev, _sublane_all(jnp.maximum, jnp.max(st, axis=0)))
            alpha = jnp.exp2(m_prev - m_new)
            p = jnp.exp2(st - m_new[None])
            l_sc[s] = alpha * l_sc[s] + jnp.sum(p, axis=0)
            pv = _dot(vt, p.reshape(size, tq).astype(BF16))
            acc = acc_sc[s].reshape(dv // SUBLANES, SUBLANES, tq) * alpha[None]
            acc_sc[s] = acc.reshape(dv, tq) + pv
            m_sc[s] = m_new

    m_sc[...] = jnp.full(m_sc.shape, NEG_BIG, F32)
    l_sc[...] = jnp.zeros(l_sc.shape, F32)
    acc_sc[...] = jnp.zeros(acc_sc.shape, F32)
    scores(sa_sc, 0, ROW_BLOCK)
    consume(sa_sc, 0, ROW_BLOCK)

    @pl.when(i > 0)
    def _():
        scores(sa_sc, ROW_BLOCK, KV_CHUNK)

    def body(jj, carry):
        c = pl.multiple_of(ROW_BLOCK + 2 * jj * KV_CHUNK, ROW_BLOCK)
        scores(sb_sc, c + KV_CHUNK, KV_CHUNK)
        consume(sa_sc, c, KV_CHUNK)
        nxt = pl.multiple_of(jnp.minimum(c + 2 * KV_CHUNK, t - KV_CHUNK), ROW_BLOCK)
        scores(sa_sc, nxt, KV_CHUNK)
        consume(sb_sc, c + KV_CHUNK, KV_CHUNK)
        return carry

    lax.fori_loop(0, jnp.where(i == 0, 0, n_lat_chunks // 2), body, 0)

    def result(s):
        l = _sublane_all(jnp.add, l_sc[s])
        o_t = acc_sc[s].reshape(dv // SUBLANES, SUBLANES, tq) / l[None]
        return o_t.reshape(dv, tq).T

    o0 = result(0)
    o1 = result(1)
    if diff:
        lv = lamv_ref[...]
        lam = (jnp.exp(jnp.sum(lv[0:1] * lv[1:2], axis=-1, keepdims=True))
               - jnp.exp(jnp.sum(lv[2:3] * lv[3:4], axis=-1, keepdims=True)) + lam_init)
        o = o0 - lam * o1
        o_ref[...] = _rms(o, subln_ref[...]) * (1.0 - lam_init)
    else:
        lane = lax.broadcasted_iota(jnp.int32, (1, LANES), 1)
        o_ref[...] = jnp.where(lane < MLA_V, o0, o1)


def _flash(q, k, vt, extra, *, diff, lam_init=0.0):
    groups, t = vt.shape[0], vt.shape[2]
    nb = t // ROW_BLOCK
    n_lat_chunks = (t - ROW_BLOCK) // KV_CHUNK
    assert n_lat_chunks % 2 == 0
    if diff:
        q_spec = pl.BlockSpec((None, 2, ROW_BLOCK, LANES), lambda g, i: (g, 0, i, 0))
        k_spec = pl.BlockSpec((None, t, LANES), lambda g, i: (g, 0, 0))
        extra_specs = [pl.BlockSpec(e.shape, lambda g, i: (0, 0)) for e in extra]
    else:
        q_spec = pl.BlockSpec((2, ROW_BLOCK, LANES), lambda g, i: (g, i, 0))
        k_spec = pl.BlockSpec((2, t, LANES), lambda g, i: (g, 0, 0))
        extra_specs = []
    return pl.pallas_call(
        functools.partial(_flash_kernel, diff=diff, lam_init=lam_init, n_lat_chunks=n_lat_chunks),
        grid=(groups, nb),
        in_specs=[q_spec, k_spec, pl.BlockSpec((None, LANES, t), lambda g, i: (g, 0, 0))] + extra_specs,
        out_specs=pl.BlockSpec((ROW_BLOCK, LANES), lambda g, i: (i, g)),
        out_shape=jax.ShapeDtypeStruct((t, groups * LANES), F32),
        scratch_shapes=[pltpu.VMEM((2, SUBLANES, ROW_BLOCK), F32), pltpu.VMEM((2, SUBLANES, ROW_BLOCK), F32),
                        pltpu.VMEM((2, LANES, ROW_BLOCK), F32),
                        pltpu.VMEM((2, KV_CHUNK, ROW_BLOCK), F32), pltpu.VMEM((2, KV_CHUNK, ROW_BLOCK), F32)],
        compiler_params=_params(("parallel", "arbitrary")),
        name="flash_diff" if diff else "flash_mla",
    )(q, k, vt, *extra)


NA_ROWS_PER_STEP = ROW_BLOCK // GRID_W


def _na_kernel(q_ref, k_ref, v_ref, bt_ref, o_ref, *, grid_rows):
    i = pl.program_id(1)
    lane = lax.broadcasted_iota(jnp.int32, (1, LANES), 1)
    first = lane < NA_HD
    kctx = k_ref[0:ROW_BLOCK, :]
    vctx = v_ref[0:ROW_BLOCK, :]

    def stacked(q):
        zero = jnp.zeros_like(q)
        return jnp.concatenate([jnp.where(first, q, zero), jnp.where(first, zero, q)], axis=0)

    @pl.when(i == 0)
    def _():
        q2 = stacked(q_ref[...])
        sc = _dot_nt(q2, kctx)
        p = jnp.exp(sc - jnp.max(sc, axis=-1, keepdims=True))
        o = _dot(p.astype(BF16), vctx) / jnp.sum(p, axis=-1, keepdims=True)
        o_ref[...] = jnp.where(first, o[0:ROW_BLOCK], o[ROW_BLOCK:])

    @pl.when(i > 0)
    def _():
        nkeys = WIN_R * GRID_W
        for rr in range(NA_ROWS_PER_STEP):
            r = (i - 1) * NA_ROWS_PER_STEP + rr
            r0 = jnp.clip(r - WIN_R // 2, 0, grid_rows - WIN_R)
            start = pl.multiple_of(ROW_BLOCK + r0 * GRID_W, GRID_W)
            kl = k_ref[pl.ds(start, nkeys), :]
            vl = v_ref[pl.ds(start, nkeys), :]
            q2 = stacked(q_ref[rr * GRID_W:(rr + 1) * GRID_W, :])
            s_lat = _dot_nt(q2, kl) + bt_ref[r0 - r + WIN_R - 1]
            s_ctx = _dot_nt(q2, kctx)
            m = jnp.maximum(jnp.max(s_lat, axis=-1, keepdims=True), jnp.max(s_ctx, axis=-1, keepdims=True))
            p_lat = jnp.exp(s_lat - m)
            p_ctx = jnp.exp(s_ctx - m)
            l = jnp.sum(p_lat, axis=-1, keepdims=True) + jnp.sum(p_ctx, axis=-1, keepdims=True)
            o = (_dot(p_lat.astype(BF16), vl) + _dot(p_ctx.astype(BF16), vctx)) / l
            o_ref[rr * GRID_W:(rr + 1) * GRID_W, :] = jnp.where(first, o[0:GRID_W], o[GRID_W:])


def _na_attention(q, k, v, bias):
    npair, t = q.shape[0], q.shape[1]
    nb = t // ROW_BLOCK
    grid_rows = (t - ROW_BLOCK) // GRID_W
    resident = pl.BlockSpec((None, t, LANES), lambda p, i: (p, 0, 0))
    return pl.pallas_call(
        functools.partial(_na_kernel, grid_rows=grid_rows),
        grid=(npair, nb),
        in_specs=[pl.BlockSpec((None, ROW_BLOCK, LANES), lambda p, i: (p, i, 0)), resident, resident,
                  pl.BlockSpec((None,) + bias.shape[1:], lambda p, i: (p, 0, 0, 0))],
        out_specs=pl.BlockSpec((ROW_BLOCK, LANES), lambda p, i: (i, p)),
        out_shape=jax.ShapeDtypeStruct((t, npair * LANES), F32),
        compiler_params=_params(("parallel", "arbitrary")),
        name="na_attention",
    )(q, k, v, bias)


def _na_bias_table(rpb):
    w = jnp.arange(GRID_W)
    c0 = jnp.clip(w - WIN_C // 2, 0, GRID_W - WIN_C)
    j = jnp.arange(GRID_W)
    inside = (j[None, :] >= c0[:, None]) & (j[None, :] < c0[:, None] + WIN_C)
    col_off = jnp.clip(j[None, :] - w[:, None] + WIN_C - 1, 0, 2 * WIN_C - 2)
    row_off = jnp.arange(WIN_R)[:, None] + jnp.arange(WIN_R)[None, :]
    tab = rpb[:, row_off][:, :, :, col_off]
    tab = jnp.where(inside[None, None, None], tab, NEG_BIG)
    tab = tab.transpose(0, 1, 3, 2, 4).reshape(NA_HEADS // 2, 2, WIN_R, GRID_W, WIN_R * GRID_W)
    return tab.transpose(0, 2, 1, 3, 4).reshape(NA_HEADS // 2, WIN_R, 2 * GRID_W, WIN_R * GRID_W)


def _out_kernel(a_ref, b_ref, gate_ref, w_ref, x_ref, mod_ref, o_ref):
    half = a_ref.shape[-1]
    d = x_ref.shape[-1]
    ga = gate_ref[:, 0:half]
    gb = gate_ref[:, half:2 * half]
    ya = (a_ref[...] * (ga * jax.nn.sigmoid(ga))).astype(BF16)
    yb = (b_ref[...] * (gb * jax.nn.sigmoid(gb))).astype(BF16)
    y = _dot(ya, w_ref[0:half, :]) + _dot(yb, w_ref[half:2 * half, :])
    o_ref[...] = x_ref[...] + mod_ref[:, 2 * d:3 * d] * y


def _out_proj(a, b, gate, w, x, mod):
    t, d = x.shape
    nb = t // ROW_BLOCK
    rows = lambda c: pl.BlockSpec((ROW_BLOCK, c), lambda i: (i, 0))
    return pl.pallas_call(
        _out_kernel,
        grid=(nb,),
        in_specs=[rows(a.shape[1]), rows(b.shape[1]), rows(gate.shape[1]),
                  pl.BlockSpec(w.shape, lambda i: (0, 0)), rows(d),
                  pl.BlockSpec((None, 1, 3 * d), lambda i: (jnp.minimum(i, 1), 0, 0))],
        out_specs=rows(d),
        out_shape=jax.ShapeDtypeStruct((t, d), F32),
        compiler_params=_params(("parallel",)),
        name="out_proj",
    )(a, b, gate, w, x, mod)


def _lru_kernel(*refs, reverse, nb):
    if reverse:
        (x_ref, xp_ref, xn_ref, cw_ref, cb_ref, lam_ref, wa_ref, ba_ref, wx_ref, bx_ref, hf_ref,
         o_ref, xs_sc, a_sc, b_sc, carry_sc) = refs
    else:
        (x_ref, xp_ref, xn_ref, cw_ref, cb_ref, lam_ref, wa_ref, ba_ref, wx_ref, bx_ref,
         o_ref, xs_sc, a_sc, b_sc, carry_sc) = refs
    step = pl.program_id(0)
    blk = jnp.where(step == 0, 0, nb - step) if reverse else step
    tb = x_ref.shape[0]

    @pl.when(step == 0)
    def _():
        carry_sc[...] = jnp.zeros_like(carry_sc)

    prev_ok = blk >= 2
    next_ok = (blk >= 1) & (blk < nb - 1)
    xs_sc[0:SUBLANES, :] = jnp.where(prev_ok, xp_ref[...], 0.0)
    xs_sc[SUBLANES:SUBLANES + tb, :] = x_ref[...]
    xs_sc[SUBLANES + tb:2 * SUBLANES + tb, :] = jnp.where(next_ok, xn_ref[...], 0.0)
    u = cb_ref[...]
    for j in range(CONV_W):
        off = SUBLANES + j - CONV_W // 2
        u = u + xs_sc[off:off + tb, :] * cw_ref[j:j + 1, :]

    ub = u.astype(BF16)
    r = jax.nn.sigmoid(_dot(ub, wa_ref[...]) + ba_ref[...])
    gi = jax.nn.sigmoid(_dot(ub, wx_ref[...]) + bx_ref[...])
    nl = -lam_ref[...]
    softplus = jnp.maximum(nl, 0.0) + jnp.log1p(jnp.exp(-jnp.abs(nl)))
    log_a = -LRU_C * r * softplus
    a_sc[...] = jnp.exp(log_a)
    b_sc[...] = jnp.sqrt(1.0 - jnp.exp(2.0 * log_a)) * (gi * u)

    row = lax.broadcasted_iota(jnp.int32, (SUBLANES, 1), 0)
    nchunk = tb // SUBLANES

    def chunk(c, carry):
        cc = nchunk - 1 - c if reverse else c
        start = pl.multiple_of(cc * SUBLANES, SUBLANES)
        a = a_sc[pl.ds(start, SUBLANES), :]
        b = b_sc[pl.ds(start, SUBLANES), :]
        for k in (1, 2, 4):
            shift = SUBLANES - k if reverse else k
            valid = (row < SUBLANES - k) if reverse else (row >= k)
            a_sh = pltpu.roll(a, shift, 0)
            b_sh = pltpu.roll(b, shift, 0)
            b = jnp.where(valid, a * b_sh + b, b)
            a = jnp.where(valid, a * a_sh, a)
        h = b + a * carry
        if reverse:
            o_ref[pl.ds(start, SUBLANES), :] = h + hf_ref[pl.ds(start, SUBLANES), :]
            new = h[0:1, :]
        else:
            o_ref[pl.ds(start, SUBLANES), :] = h
            new = h[SUBLANES - 1:SUBLANES, :]
        return jnp.broadcast_to(new, carry.shape)

    carry_sc[...] = lax.fori_loop(0, nchunk, chunk, carry_sc[...])


def _lru_pass(x, conv_w, conv_b, lam, wa, ba, wx, bx, hf, *, reverse):
    t, c = x.shape
    tb = ROW_BLOCK
    nb = t // tb
    per8 = tb // SUBLANES
    nb8 = t // SUBLANES
    if reverse:
        blk = lambda s: jnp.where(s == 0, 0, nb - s)
    else:
        blk = lambda s: s
    full = lambda a: pl.BlockSpec(a.shape, lambda s: (0,) * a.ndim)
    main = pl.BlockSpec((tb, c), lambda s: (blk(s), 0))
    args = [x, x, x, conv_w, conv_b, lam, wa, ba, wx, bx]
    in_specs = [main,
                pl.BlockSpec((SUBLANES, c), lambda s: (jnp.maximum(blk(s) * per8 - 1, 0), 0)),
                pl.BlockSpec((SUBLANES, c), lambda s: (jnp.minimum((blk(s) + 1) * per8, nb8 - 1), 0)),
                full(conv_w), full(conv_b), full(lam), full(wa), full(ba), full(wx), full(bx)]
    if reverse:
        args.append(hf)
        in_specs.append(main)
    return pl.pallas_call(
        functools.partial(_lru_kernel, reverse=reverse, nb=nb),
        grid=(nb,),
        in_specs=in_specs,
        out_specs=main,
        out_shape=jax.ShapeDtypeStruct((t, c), F32),
        scratch_shapes=[pltpu.VMEM((tb + 2 * SUBLANES, c), F32), pltpu.VMEM((tb, c), F32),
                        pltpu.VMEM((tb, c), F32), pltpu.VMEM((SUBLANES, c), F32)],
        compiler_params=_params(("arbitrary",)),
        name="lru_bwd" if reverse else "lru_fwd",
    )(*args)


def _block_diag(w):
    g, bw, _ = w.shape
    eye = jnp.eye(g, dtype=w.dtype)
    return (eye[:, None, :, None] * w[:, :, None, :]).reshape(g * bw, g * bw)


def _final_kernel(x_ref, g_ref, o_ref):
    o_ref[...] = _rms(x_ref[...], g_ref[...])


def _final_norm(x, g, n):
    t, d = x.shape
    skip = (t - n) // ROW_BLOCK
    return pl.pallas_call(
        _final_kernel,
        grid=(n // ROW_BLOCK,),
        in_specs=[pl.BlockSpec((ROW_BLOCK, d), lambda i: (i + skip, 0)), pl.BlockSpec((1, d), lambda i: (0, 0))],
        out_specs=pl.BlockSpec((ROW_BLOCK, d), lambda i: (i, 0)),
        out_shape=jax.ShapeDtypeStruct((n, d), F32),
        compiler_params=_params(("parallel",)),
        name="final_norm",
    )(x, g)


def _rope_tables(n, ctx_len, rot_dim, lead, tail, repeat):
    t = jnp.arange(n)
    row = (t // GRID_W).astype(F32)
    col = (t % GRID_W).astype(F32)
    n_freq = rot_dim // 4
    inv = ROPE_BASE ** (-jnp.arange(n_freq, dtype=F32) / n_freq)
    ang = jnp.concatenate([row[:, None] * inv, col[:, None] * inv], axis=-1)
    cos, sin = jnp.cos(ang), jnp.sin(ang)
    cos_l = jnp.concatenate([jnp.ones((n, lead), F32)] + [cos, cos] * repeat + [jnp.ones((n, tail), F32)], axis=-1)
    sin_l = jnp.concatenate([jnp.zeros((n, lead), F32)] + [-sin, sin] * repeat + [jnp.zeros((n, tail), F32)], axis=-1)
    cos_t = jnp.concatenate([jnp.ones((ctx_len, LANES), F32), cos_l], axis=0)
    sin_t = jnp.concatenate([jnp.zeros((ctx_len, LANES), F32), sin_l], axis=0)
    return cos_t, sin_t


def _even_weights(w_in, w_uq, w_ukv):
    d = w_in.shape[0]
    q, k, v, g_na, lq, lkv, kr, g_mla = jnp.split(
        w_in, [512, 1024, 1536, 2048, 2048 + Q_LORA, 2048 + Q_LORA + KV_LORA, 2048 + Q_LORA + KV_LORA + MLA_ROPE], axis=1)
    half = MLA_ROPE // 2
    pad = LANES - MLA_QK
    kr_pad = jnp.concatenate([jnp.zeros((d, MLA_NOPE), F32), kr, jnp.zeros((d, pad), F32)], axis=1)
    krs_pad = jnp.concatenate([jnp.zeros((d, MLA_NOPE), F32), kr[:, half:], kr[:, :half], jnp.zeros((d, pad), F32)], axis=1)
    w = jnp.concatenate([q * (NA_HD ** -0.5), k, v, g_na, g_mla, lq, lkv, kr_pad, krs_pad], axis=1).astype(BF16)
    uq = w_uq.reshape(Q_LORA, MLA_HEADS, MLA_QK)
    zq = jnp.zeros((Q_LORA, MLA_HEADS, pad), F32)
    uq_main = jnp.concatenate([uq, zq], axis=-1)
    uq_swap = jnp.concatenate([jnp.zeros((Q_LORA, MLA_HEADS, MLA_NOPE), F32), uq[..., MLA_NOPE + half:],
                               uq[..., MLA_NOPE:MLA_NOPE + half], zq], axis=-1)
    wuq = jnp.concatenate([uq_main.reshape(Q_LORA, -1), uq_swap.reshape(Q_LORA, -1)], axis=1).astype(BF16)
    ukv = w_ukv.reshape(KV_LORA, MLA_HEADS, MLA_NOPE + MLA_V)
    k_nope = jnp.concatenate([ukv[..., :MLA_NOPE], jnp.zeros((KV_LORA, MLA_HEADS, LANES - MLA_NOPE), F32)], axis=-1)
    wkv = jnp.concatenate([k_nope.reshape(KV_LORA, -1), ukv[..., MLA_NOPE:].reshape(KV_LORA, -1)], axis=1).astype(BF16)
    return w, wuq, wkv


def _odd_weights(w_in):
    d = w_in.shape[0]
    q, k, v, g_d, u, g_lru = jnp.split(w_in, [512, 1024, 1536, 2048, 2560], axis=1)

    def swap(m):
        return m.reshape(d, DIFF_HEADS, 2, 2, DIFF_HD // 2)[:, :, :, ::-1, :].reshape(d, DIFF_W)

    scale = DIFF_HD ** -0.5
    return jnp.concatenate([q * scale, swap(q) * scale, k, swap(k), v, g_d, g_lru, u], axis=1).astype(BF16)


def kernel(x, c, ctx, c_ctx, mod_w, mod_b, norm_g, final_g, e_w_in, e_w_out, na_rpb, mla_q_norm, mla_w_uq, mla_kv_norm, mla_w_ukv, o_w_in, o_w_out, diff_lq1, diff_lk1, diff_lq2, diff_lk2, diff_subln, lru_conv_w, lru_conv_b, lru_lambda, lru_wa, lru_ba, lru_wx, lru_bx):
    batch, n, d = x.shape
    ctx_len = ctx.shape[1]
    assert batch == 1 and ctx_len == ROW_BLOCK and d == D_MODEL
    assert n % KV_CHUNK == 0 and n // GRID_W >= WIN_R

    cvec = jnp.concatenate([c_ctx[None], c, jnp.zeros((SUBLANES - 2, d), F32)], axis=0)
    mods = _modulation(cvec, mod_w, mod_b)[:, 0:2].reshape(DEPTH, 2, 1, 3 * d)

    cos_m, sin_m = _rope_tables(n, ctx_len, MLA_ROPE, MLA_NOPE, LANES - MLA_QK, 1)
    cos_d, sin_d = _rope_tables(n, ctx_len, DIFF_HD, 0, 0, 2)

    xs = jnp.concatenate([ctx[0], x[0]], axis=0)
    for l in range(DEPTH):
        i = l // 2
        g_row = norm_g[l][None]
        if l % 2 == 0:
            w, wuq, wkv = _even_weights(e_w_in[i], mla_w_uq[i], mla_w_ukv[i])
            naq, nak, nav, gate, mq, mk, mv = _in_even(
                xs, g_row, mods[l], w, mla_q_norm[i][None], wuq, mla_kv_norm[i][None], wkv, cos_m, sin_m)
            a = _na_attention(naq, nak, nav, _na_bias_table(na_rpb[i]))
            b = _flash(mq, mk, mv, (), diff=False)
            w_out = e_w_out[i]
        else:
            lam_init = 0.8 - 0.6 * math.exp(-0.3 * l)
            dq, dk, dv, gate, u = _in_odd(xs, g_row, mods[l], _odd_weights(o_w_in[i]), cos_d, sin_d)
            lamv = jnp.stack([diff_lq1[i], diff_lk1[i], diff_lq2[i], diff_lk2[i]])
            a = _flash(dq, dk, dv, (lamv, diff_subln[i][None]), diff=True, lam_init=lam_init)
            lru = lambda dr, hf: _lru_pass(
                u, lru_conv_w[i], lru_conv_b[i][None], lru_lambda[i, dr][None],
                _block_diag(lru_wa[i, dr]).astype(BF16), lru_ba[i, dr][None],
                _block_diag(lru_wx[i, dr]).astype(BF16), lru_bx[i, dr][None], hf, reverse=dr == 1)
            b = lru(1, lru(0, None))
            w_out = o_w_out[i]
        xs = _out_proj(a, b, gate, w_out.astype(BF16), xs, mods[l])
    return _final_norm(xs, final_g[None], n)[None]
```

```python
import functools
import math

import jax
import jax.numpy as jnp
from jax import lax
from jax.experimental import pallas as pl
from jax.experimental.pallas import tpu as pltpu

F32 = jnp.float32
BF16 = jnp.bfloat16

D_MODEL = 1024
DEPTH = 4
GRID_W = 64
EPS = 1e-6
ROPE_BASE = 10000.0

NA_HEADS = 8
NA_HD = 64
NA_W = NA_HEADS * NA_HD
WIN_R = 8
WIN_C = 16

MLA_HEADS = 8
MLA_NOPE = 64
MLA_ROPE = 32
MLA_V = 64
MLA_QK = MLA_NOPE + MLA_ROPE
MLA_W = MLA_HEADS * MLA_V
Q_LORA = 384
KV_LORA = 256

DIFF_HEADS = 4
DIFF_HD = 64
DIFF_W = DIFF_HEADS * 2 * DIFF_HD

LRU_W = 512
LRU_BLOCKS = 8
LRU_BW = LRU_W // LRU_BLOCKS
CONV_W = 4
LRU_C = 8.0

LANES = 128
SUBLANES = 8
ROW_BLOCK = 256
KV_CHUNK = 1024
SOFTMAX_ROWS = 128
FLASH_UNROLL = 4
VMEM_LIMIT = 56 * 1024 * 1024
NEG_BIG = -1e30
LOG2E = math.log2(math.e)

_NT = (((1,), (1,)), ((), ()))


def _dot_nt(a, b):
    return lax.dot_general(a, b, _NT, preferred_element_type=F32)


def _dot(a, b):
    return jnp.dot(a, b, preferred_element_type=F32)


def _params(sem, flags=None):
    return pltpu.CompilerParams(dimension_semantics=sem, vmem_limit_bytes=VMEM_LIMIT, flags=flags)


def _mod_kernel(c_ref, w_ref, b_ref, o_ref):
    c = c_ref[...]
    s = c * jax.nn.sigmoid(c)
    o_ref[...] = _dot(s.astype(BF16), w_ref[...].astype(BF16)) + b_ref[...]


def _modulation(cvec, mod_w, mod_b):
    depth, d, d3 = mod_w.shape
    tn = 512
    return pl.pallas_call(
        _mod_kernel,
        grid=(depth, d3 // tn),
        in_specs=[
            pl.BlockSpec((SUBLANES, d), lambda l, j: (0, 0)),
            pl.BlockSpec((None, d, tn), lambda l, j: (l, 0, j)),
            pl.BlockSpec((None, 1, tn), lambda l, j: (l, 0, j)),
        ],
        out_specs=pl.BlockSpec((None, SUBLANES, tn), lambda l, j: (l, 0, j)),
        out_shape=jax.ShapeDtypeStruct((depth, SUBLANES, d3), F32),
        compiler_params=_params(("parallel", "parallel")),
        name="modulation",
    )(cvec, mod_w, mod_b.reshape(depth, 1, d3))


def _norm_mod(x, g, mod):
    d = x.shape[-1]
    y = x * lax.rsqrt(jnp.mean(x * x, axis=-1, keepdims=True) + EPS) * g
    return y * (1.0 + mod[:, d:2 * d]) + mod[:, 0:d]


def _rms(x, g):
    return x * lax.rsqrt(jnp.mean(x * x, axis=-1, keepdims=True) + EPS) * g


E_Q, E_K, E_V, E_G, E_LQ, E_LKV, E_KR, E_KRS, E_END = 0, 512, 1024, 1536, 2560, 2944, 3200, 3328, 3456


def _in_even_kernel(x_ref, g_ref, mod_ref, w_ref, qn_ref, wuq_ref, kvn_ref, wkv_ref, c_ref, s_ref,
                    naq_ref, nak_ref, nav_ref, gate_ref, mq_ref, mk_ref, mv_ref):
    hb = _norm_mod(x_ref[...], g_ref[...], mod_ref[...]).astype(BF16)
    npair = NA_HEADS // 2
    for ref, off in ((naq_ref, E_Q), (nak_ref, E_K), (nav_ref, E_V)):
        seg = _dot(hb, w_ref[:, off:off + NA_W])
        for p in range(npair):
            ref[p] = seg[:, p * LANES:(p + 1) * LANES].astype(BF16)
    gate_ref[...] = _dot(hb, w_ref[:, E_G:E_LQ])
    cos = c_ref[...]
    sin = s_ref[...]
    qn = _rms(_dot(hb, w_ref[:, E_LQ:E_LKV]), qn_ref[...]).astype(BF16)
    width = MLA_HEADS * LANES
    q_main = _dot(qn, wuq_ref[:, 0:width])
    q_swap = _dot(qn, wuq_ref[:, width:2 * width])
    scale = MLA_QK ** -0.5 * LOG2E
    for h in range(MLA_HEADS):
        sl = slice(h * LANES, (h + 1) * LANES)
        mq_ref[h] = ((q_main[:, sl] * cos + q_swap[:, sl] * sin) * scale).astype(BF16)
    kvn = _rms(_dot(hb, w_ref[:, E_LKV:E_KR]), kvn_ref[...]).astype(BF16)
    kr = _dot(hb, w_ref[:, E_KR:E_KRS]) * cos + _dot(hb, w_ref[:, E_KRS:E_END]) * sin
    k_nope = _dot(kvn, wkv_ref[:, 0:width])
    for h in range(MLA_HEADS):
        mk_ref[h] = (k_nope[:, h * LANES:(h + 1) * LANES] + kr).astype(BF16)
    v = _dot(kvn, wkv_ref[:, width:width + MLA_W])
    for p in range(MLA_HEADS // 2):
        mv_ref[p] = v[:, p * LANES:(p + 1) * LANES].T.astype(BF16)


def _in_even(x, norm_g, mod, w, qn, wuq, kvn, wkv, cos, sin):
    t, d = x.shape
    nb = t // ROW_BLOCK
    full = lambda a: pl.BlockSpec(a.shape, lambda i: (0,) * a.ndim)
    rows = lambda c: pl.BlockSpec((ROW_BLOCK, c), lambda i: (i, 0))
    heads = lambda n: pl.BlockSpec((n, ROW_BLOCK, LANES), lambda i: (0, i, 0))
    hshape = lambda n: jax.ShapeDtypeStruct((n, t, LANES), BF16)
    return pl.pallas_call(
        _in_even_kernel,
        grid=(nb,),
        in_specs=[rows(d), full(norm_g),
                  pl.BlockSpec((None, 1, 3 * d), lambda i: (jnp.minimum(i, 1), 0, 0)),
                  full(w), full(qn), full(wuq), full(kvn), full(wkv), rows(LANES), rows(LANES)],
        out_specs=[heads(4), heads(4), heads(4), rows(2 * NA_W), heads(8), heads(8),
                   pl.BlockSpec((4, LANES, ROW_BLOCK), lambda i: (0, 0, i))],
        out_shape=[hshape(4), hshape(4), hshape(4), jax.ShapeDtypeStruct((t, 2 * NA_W), F32),
                   hshape(8), hshape(8), jax.ShapeDtypeStruct((4, LANES, t), BF16)],
        compiler_params=_params(("parallel",)),
        name="in_proj_even",
    )(x, norm_g, mod, w, qn, wuq, kvn, wkv, cos, sin)


O_Q, O_QS, O_K, O_KS, O_V, O_G, O_U, O_END = 0, 512, 1024, 1536, 2048, 2560, 3584, 4096


def _in_odd_kernel(x_ref, g_ref, mod_ref, w_ref, c_ref, s_ref, dq_ref, dk_ref, dv_ref, gate_ref, u_ref):
    hb = _norm_mod(x_ref[...], g_ref[...], mod_ref[...]).astype(BF16)
    cos = c_ref[...]
    sin = s_ref[...]
    lane = lax.broadcasted_iota(jnp.int32, (1, LANES), 1)
    first = lane < DIFF_HD
    q = _dot(hb, w_ref[:, O_Q:O_QS])
    qs = _dot(hb, w_ref[:, O_QS:O_K])
    k = _dot(hb, w_ref[:, O_K:O_KS])
    ks = _dot(hb, w_ref[:, O_KS:O_V])
    v = _dot(hb, w_ref[:, O_V:O_G])
    for h in range(DIFF_HEADS):
        sl = slice(h * LANES, (h + 1) * LANES)
        qr = (q[:, sl] * cos + qs[:, sl] * sin) * LOG2E
        dq_ref[h, 0] = jnp.where(first, qr, 0.0).astype(BF16)
        dq_ref[h, 1] = jnp.where(first, 0.0, qr).astype(BF16)
        dk_ref[h] = (k[:, sl] * cos + ks[:, sl] * sin).astype(BF16)
        dv_ref[h] = v[:, sl].T.astype(BF16)
    gate_ref[...] = _dot(hb, w_ref[:, O_G:O_U])
    u_ref[...] = _dot(hb, w_ref[:, O_U:O_END])


def _in_odd(x, norm_g, mod, w, cos, sin):
    t, d = x.shape
    nb = t // ROW_BLOCK
    full = lambda a: pl.BlockSpec(a.shape, lambda i: (0,) * a.ndim)
    rows = lambda c: pl.BlockSpec((ROW_BLOCK, c), lambda i: (i, 0))
    heads = pl.BlockSpec((DIFF_HEADS, ROW_BLOCK, LANES), lambda i: (0, i, 0))
    hshape = jax.ShapeDtypeStruct((DIFF_HEADS, t, LANES), BF16)
    return pl.pallas_call(
        _in_odd_kernel,
        grid=(nb,),
        in_specs=[rows(d), full(norm_g),
                  pl.BlockSpec((None, 1, 3 * d), lambda i: (jnp.minimum(i, 1), 0, 0)),
                  full(w), rows(LANES), rows(LANES)],
        out_specs=[pl.BlockSpec((DIFF_HEADS, 2, ROW_BLOCK, LANES), lambda i: (0, 0, i, 0)),
                   heads, pl.BlockSpec((DIFF_HEADS, LANES, ROW_BLOCK), lambda i: (0, 0, i)),
                   rows(DIFF_W + LRU_W), rows(LRU_W)],
        out_shape=[jax.ShapeDtypeStruct((DIFF_HEADS, 2, t, LANES), BF16), hshape,
                   jax.ShapeDtypeStruct((DIFF_HEADS, LANES, t), BF16),
                   jax.ShapeDtypeStruct((t, DIFF_W + LRU_W), F32), jax.ShapeDtypeStruct((t, LRU_W), F32)],
        compiler_params=_params(("parallel",)),
        name="in_proj_odd",
    )(x, norm_g, mod, w, cos, sin)


def _sublane_all(op, x):
    for k in (4, 2, 1):
        x = op(x, pltpu.roll(x, k, 0))
    return x


def _flash_kernel(*refs, diff, lam_init, n_iters):
    if diff:
        q_ref, k_ref, vt_ref, lamv_ref, subln_ref, o_ref = refs[:6]
    else:
        q_ref, k_ref, vt_ref, o_ref = refs[:4]
    m_sc, l_sc, acc_sc, s0_sc, s1_sc, p0_sc, p1_sc, a0_sc, a1_sc, c0_sc, c1_sc = refs[-11:]
    i = pl.program_id(1)
    tq = q_ref.shape[1]
    dv = vt_ref.shape[0]
    t = vt_ref.shape[1]

    def scores(s_sc, c_sc, start, size):
        for s in range(2):
            kc = k_ref[pl.ds(start, size), :] if diff else k_ref[s, pl.ds(start, size), :]
            st = _dot_nt(kc, q_ref[s])
            s_sc[s, 0:size, :] = st
            c_sc[s] = jnp.max(st.reshape(size // SUBLANES, SUBLANES, tq), axis=0)

    def softmax(s_sc, c_sc, p_sc, a_sc, size):
        for s in range(2):
            m_prev = m_sc[s]
            m_new = jnp.maximum(m_prev, _sublane_all(jnp.maximum, c_sc[s]))
            alpha = jnp.exp2(m_prev - m_new)
            l = alpha * l_sc[s]
            for r in range(0, size, SOFTMAX_ROWS):
                x = s_sc[s, r:r + SOFTMAX_ROWS, :].reshape(SOFTMAX_ROWS // SUBLANES, SUBLANES, tq) - m_new[None]
                p = jnp.exp2(x)
                l = l + jnp.sum(p, axis=0)
                p_sc[s, r:r + SOFTMAX_ROWS, :] = p.reshape(SOFTMAX_ROWS, tq).astype(BF16)
            l_sc[s] = l
            a_sc[s] = alpha
            m_sc[s] = m_new

    def accumulate(p_sc, a_sc, start):
        vt = vt_ref[:, pl.ds(start, KV_CHUNK)]
        for s in range(2):
            acc = acc_sc[s].reshape(dv // SUBLANES, SUBLANES, tq) * a_sc[s][None]
            acc_sc[s] = acc.reshape(dv, tq) + _dot(vt, p_sc[s])

    m_sc[...] = jnp.full(m_sc.shape, NEG_BIG, F32)
    l_sc[...] = jnp.zeros(l_sc.shape, F32)
    acc_sc[...] = jnp.zeros(acc_sc.shape, F32)
    scores(s0_sc, c0_sc, 0, ROW_BLOCK)
    p0_sc[:, ROW_BLOCK:, :] = jnp.zeros((2, KV_CHUNK - ROW_BLOCK, tq), BF16)
    softmax(s0_sc, c0_sc, p0_sc, a0_sc, ROW_BLOCK)

    scores(s1_sc, c1_sc, ROW_BLOCK, KV_CHUNK)
    bufs = ((s0_sc, c0_sc, p0_sc, a0_sc), (s1_sc, c1_sc, p1_sc, a1_sc))

    def body(jj, carry):
        base = ROW_BLOCK + FLASH_UNROLL * jj * KV_CHUNK
        for u in range(FLASH_UNROLL):
            s_a, c_a, p_a, a_a = bufs[u % 2]
            s_b, c_b, p_b, a_b = bufs[(u + 1) % 2]
            start = base + u * KV_CHUNK
            accumulate(p_a, a_a, pl.multiple_of(jnp.maximum(start - KV_CHUNK, 0), ROW_BLOCK))
            scores(s_a, c_a, pl.multiple_of(jnp.minimum(start + KV_CHUNK, t - KV_CHUNK), ROW_BLOCK), KV_CHUNK)
            softmax(s_b, c_b, p_b, a_b, KV_CHUNK)
        return carry

    lax.fori_loop(0, jnp.where(i == 0, 0, n_iters), body, 0)
    accumulate(p0_sc, a0_sc, jnp.where(i == 0, 0, t - KV_CHUNK))

    def result(s):
        l = _sublane_all(jnp.add, l_sc[s])
        o_t = acc_sc[s].reshape(dv // SUBLANES, SUBLANES, tq) / l[None]
        return o_t.reshape(dv, tq).T

    o0 = result(0)
    o1 = result(1)
    if diff:
        lv = lamv_ref[...]
        lam = (jnp.exp(jnp.sum(lv[0:1] * lv[1:2], axis=-1, keepdims=True))
               - jnp.exp(jnp.sum(lv[2:3] * lv[3:4], axis=-1, keepdims=True)) + lam_init)
        o = o0 - lam * o1
        o_ref[...] = _rms(o, subln_ref[...]) * (1.0 - lam_init)
    else:
        lane = lax.broadcasted_iota(jnp.int32, (1, LANES), 1)
        o_ref[...] = jnp.where(lane < MLA_V, o0, o1)


def _flash(q, k, vt, extra, *, diff, lam_init=0.0):
    groups, t = vt.shape[0], vt.shape[2]
    nb = t // ROW_BLOCK
    assert (t - ROW_BLOCK) % (FLASH_UNROLL * KV_CHUNK) == 0
    n_iters = (t - ROW_BLOCK) // (FLASH_UNROLL * KV_CHUNK)
    if diff:
        q_spec = pl.BlockSpec((None, 2, ROW_BLOCK, LANES), lambda g, i: (g, 0, i, 0))
        k_spec = pl.BlockSpec((None, t, LANES), lambda g, i: (g, 0, 0))
        extra_specs = [pl.BlockSpec(e.shape, lambda g, i: (0, 0)) for e in extra]
    else:
        q_spec = pl.BlockSpec((2, ROW_BLOCK, LANES), lambda g, i: (g, i, 0))
        k_spec = pl.BlockSpec((2, t, LANES), lambda g, i: (g, 0, 0))
        extra_specs = []
    return pl.pallas_call(
        functools.partial(_flash_kernel, diff=diff, lam_init=lam_init, n_iters=n_iters),
        grid=(groups, nb),
        in_specs=[q_spec, k_spec, pl.BlockSpec((None, LANES, t), lambda g, i: (g, 0, 0))] + extra_specs,
        out_specs=pl.BlockSpec((ROW_BLOCK, LANES), lambda g, i: (i, g)),
        out_shape=jax.ShapeDtypeStruct((t, groups * LANES), F32),
        scratch_shapes=[pltpu.VMEM((2, SUBLANES, ROW_BLOCK), F32), pltpu.VMEM((2, SUBLANES, ROW_BLOCK), F32),
                        pltpu.VMEM((2, LANES, ROW_BLOCK), F32)]
        + [pltpu.VMEM((2, KV_CHUNK, ROW_BLOCK), F32)] * 2 + [pltpu.VMEM((2, KV_CHUNK, ROW_BLOCK), BF16)] * 2
        + [pltpu.VMEM((2, SUBLANES, ROW_BLOCK), F32)] * 4,
        compiler_params=_params(("parallel", "arbitrary")),
        name="flash_diff" if diff else "flash_mla",
    )(q, k, vt, *extra)


NA_ROWS_PER_STEP = ROW_BLOCK // GRID_W


def _na_kernel(q_ref, k_ref, v_ref, bt_ref, o_ref, *, grid_rows):
    i = pl.program_id(1)
    lane = lax.broadcasted_iota(jnp.int32, (1, LANES), 1)
    first = lane < NA_HD
    kctx = k_ref[0:ROW_BLOCK, :]
    vctx = v_ref[0:ROW_BLOCK, :]

    def stacked(q):
        zero = jnp.zeros_like(q)
        return jnp.concatenate([jnp.where(first, q, zero), jnp.where(first, zero, q)], axis=0)

    @pl.when(i == 0)
    def _():
        q2 = stacked(q_ref[...])
        sc = _dot_nt(q2, kctx)
        p = jnp.exp(sc - jnp.max(sc, axis=-1, keepdims=True))
        o = _dot(p.astype(BF16), vctx) / jnp.sum(p, axis=-1, keepdims=True)
        o_ref[...] = jnp.where(first, o[0:ROW_BLOCK], o[ROW_BLOCK:])

    @pl.when(i > 0)
    def _():
        nkeys = WIN_R * GRID_W
        for rr in range(NA_ROWS_PER_STEP):
            r = (i - 1) * NA_ROWS_PER_STEP + rr
            r0 = jnp.clip(r - WIN_R // 2, 0, grid_rows - WIN_R)
            start = pl.multiple_of(ROW_BLOCK + r0 * GRID_W, GRID_W)
            kl = k_ref[pl.ds(start, nkeys), :]
            vl = v_ref[pl.ds(start, nkeys), :]
            q2 = stacked(q_ref[rr * GRID_W:(rr + 1) * GRID_W, :])
            s_lat = _dot_nt(q2, kl) + bt_ref[r0 - r + WIN_R - 1]
            s_ctx = _dot_nt(q2, kctx)
            m = jnp.maximum(jnp.max(s_lat, axis=-1, keepdims=True), jnp.max(s_ctx, axis=-1, keepdims=True))
            p_lat = jnp.exp(s_lat - m)
            p_ctx = jnp.exp(s_ctx - m)
            l = jnp.sum(p_lat, axis=-1, keepdims=True) + jnp.sum(p_ctx, axis=-1, keepdims=True)
            o = (_dot(p_lat.astype(BF16), vl) + _dot(p_ctx.astype(BF16), vctx)) / l
            o_ref[rr * GRID_W:(rr + 1) * GRID_W, :] = jnp.where(first, o[0:GRID_W], o[GRID_W:])


def _na_attention(q, k, v, bias):
    npair, t = q.shape[0], q.shape[1]
    nb = t // ROW_BLOCK
    grid_rows = (t - ROW_BLOCK) // GRID_W
    resident = pl.BlockSpec((None, t, LANES), lambda p, i: (p, 0, 0))
    return pl.pallas_call(
        functools.partial(_na_kernel, grid_rows=grid_rows),
        grid=(npair, nb),
        in_specs=[pl.BlockSpec((None, ROW_BLOCK, LANES), lambda p, i: (p, i, 0)), resident, resident,
                  pl.BlockSpec((None,) + bias.shape[1:], lambda p, i: (p, 0, 0, 0))],
        out_specs=pl.BlockSpec((ROW_BLOCK, LANES), lambda p, i: (i, p)),
        out_shape=jax.ShapeDtypeStruct((t, npair * LANES), F32),
        compiler_params=_params(("parallel", "arbitrary")),
        name="na_attention",
    )(q, k, v, bias)


def _na_bias_table(rpb):
    w = jnp.arange(GRID_W)
    c0 = jnp.clip(w - WIN_C // 2, 0, GRID_W - WIN_C)
    j = jnp.arange(GRID_W)
    inside = (j[None, :] >= c0[:, None]) & (j[None, :] < c0[:, None] + WIN_C)
    col_off = jnp.clip(j[None, :] - w[:, None] + WIN_C - 1, 0, 2 * WIN_C - 2)
    row_off = jnp.arange(WIN_R)[:, None] + jnp.arange(WIN_R)[None, :]
    tab = rpb[:, row_off][:, :, :, col_off]
    tab = jnp.where(inside[None, None, None], tab, NEG_BIG)
    tab = tab.transpose(0, 1, 3, 2, 4).reshape(NA_HEADS // 2, 2, WIN_R, GRID_W, WIN_R * GRID_W)
    return tab.transpose(0, 2, 1, 3, 4).reshape(NA_HEADS // 2, WIN_R, 2 * GRID_W, WIN_R * GRID_W)


def _out_kernel(a_ref, b_ref, gate_ref, w_ref, x_ref, mod_ref, o_ref):
    half = a_ref.shape[-1]
    d = x_ref.shape[-1]
    ga = gate_ref[:, 0:half]
    gb = gate_ref[:, half:2 * half]
    ya = (a_ref[...] * (ga * jax.nn.sigmoid(ga))).astype(BF16)
    yb = (b_ref[...] * (gb * jax.nn.sigmoid(gb))).astype(BF16)
    y = _dot(ya, w_ref[0:half, :]) + _dot(yb, w_ref[half:2 * half, :])
    o_ref[...] = x_ref[...] + mod_ref[:, 2 * d:3 * d] * y


def _out_proj(a, b, gate, w, x, mod):
    t, d = x.shape
    nb = t // ROW_BLOCK
    rows = lambda c: pl.BlockSpec((ROW_BLOCK, c), lambda i: (i, 0))
    return pl.pallas_call(
        _out_kernel,
        grid=(nb,),
        in_specs=[rows(a.shape[1]), rows(b.shape[1]), rows(gate.shape[1]),
                  pl.BlockSpec(w.shape, lambda i: (0, 0)), rows(d),
                  pl.BlockSpec((None, 1, 3 * d), lambda i: (jnp.minimum(i, 1), 0, 0))],
        out_specs=rows(d),
        out_shape=jax.ShapeDtypeStruct((t, d), F32),
        compiler_params=_params(("parallel",)),
        name="out_proj",
    )(a, b, gate, w, x, mod)


def _lru_kernel(*refs, reverse, nb):
    if reverse:
        (x_ref, xp_ref, xn_ref, cw_ref, cb_ref, lam_ref, wa_ref, ba_ref, wx_ref, bx_ref, hf_ref,
         o_ref, xs_sc, a_sc, b_sc, carry_sc) = refs
    else:
        (x_ref, xp_ref, xn_ref, cw_ref, cb_ref, lam_ref, wa_ref, ba_ref, wx_ref, bx_ref,
         o_ref, xs_sc, a_sc, b_sc, carry_sc) = refs
    step = pl.program_id(0)
    blk = jnp.where(step == 0, 0, nb - step) if reverse else step
    tb = x_ref.shape[0]

    @pl.when(step == 0)
    def _():
        carry_sc[...] = jnp.zeros_like(carry_sc)

    prev_ok = blk >= 2
    next_ok = (blk >= 1) & (blk < nb - 1)
    xs_sc[0:SUBLANES, :] = jnp.where(prev_ok, xp_ref[...], 0.0)
    xs_sc[SUBLANES:SUBLANES + tb, :] = x_ref[...]
    xs_sc[SUBLANES + tb:2 * SUBLANES + tb, :] = jnp.where(next_ok, xn_ref[...], 0.0)
    u = cb_ref[...]
    for j in range(CONV_W):
        off = SUBLANES + j - CONV_W // 2
        u = u + xs_sc[off:off + tb, :] * cw_ref[j:j + 1, :]

    ub = u.astype(BF16)
    r = jax.nn.sigmoid(_dot(ub, wa_ref[...]) + ba_ref[...])
    gi = jax.nn.sigmoid(_dot(ub, wx_ref[...]) + bx_ref[...])
    nl = -lam_ref[...]
    softplus = jnp.maximum(nl, 0.0) + jnp.log1p(jnp.exp(-jnp.abs(nl)))
    log_a = -LRU_C * r * softplus
    a_sc[...] = jnp.exp(log_a)
    b_sc[...] = jnp.sqrt(1.0 - jnp.exp(2.0 * log_a)) * (gi * u)

    row = lax.broadcasted_iota(jnp.int32, (SUBLANES, 1), 0)
    nchunk = tb // SUBLANES

    def chunk(c, carry):
        cc = nchunk - 1 - c if reverse else c
        start = pl.multiple_of(cc * SUBLANES, SUBLANES)
        a = a_sc[pl.ds(start, SUBLANES), :]
        b = b_sc[pl.ds(start, SUBLANES), :]
        for k in (1, 2, 4):
            shift = SUBLANES - k if reverse else k
            valid = (row < SUBLANES - k) if reverse else (row >= k)
            a_sh = pltpu.roll(a, shift, 0)
            b_sh = pltpu.roll(b, shift, 0)
            b = jnp.where(valid, a * b_sh + b, b)
            a = jnp.where(valid, a * a_sh, a)
        h = b + a * carry
        if reverse:
            o_ref[pl.ds(start, SUBLANES), :] = h + hf_ref[pl.ds(start, SUBLANES), :]
            new = h[0:1, :]
        else:
            o_ref[pl.ds(start, SUBLANES), :] = h
            new = h[SUBLANES - 1:SUBLANES, :]
        return jnp.broadcast_to(new, carry.shape)

    carry_sc[...] = lax.fori_loop(0, nchunk, chunk, carry_sc[...])


def _lru_pass(x, conv_w, conv_b, lam, wa, ba, wx, bx, hf, *, reverse):
    t, c = x.shape
    tb = ROW_BLOCK
    nb = t // tb
    per8 = tb // SUBLANES
    nb8 = t // SUBLANES
    if reverse:
        blk = lambda s: jnp.where(s == 0, 0, nb - s)
    else:
        blk = lambda s: s
    full = lambda a: pl.BlockSpec(a.shape, lambda s: (0,) * a.ndim)
    main = pl.BlockSpec((tb, c), lambda s: (blk(s), 0))
    args = [x, x, x, conv_w, conv_b, lam, wa, ba, wx, bx]
    in_specs = [main,
                pl.BlockSpec((SUBLANES, c), lambda s: (jnp.maximum(blk(s) * per8 - 1, 0), 0)),
                pl.BlockSpec((SUBLANES, c), lambda s: (jnp.minimum((blk(s) + 1) * per8, nb8 - 1), 0)),
                full(conv_w), full(conv_b), full(lam), full(wa), full(ba), full(wx), full(bx)]
    if reverse:
        args.append(hf)
        in_specs.append(main)
    return pl.pallas_call(
        functools.partial(_lru_kernel, reverse=reverse, nb=nb),
        grid=(nb,),
        in_specs=in_specs,
        out_specs=main,
        out_shape=jax.ShapeDtypeStruct((t, c), F32),
        scratch_shapes=[pltpu.VMEM((tb + 2 * SUBLANES, c), F32), pltpu.VMEM((tb, c), F32),
                        pltpu.VMEM((tb, c), F32), pltpu.VMEM((SUBLANES, c), F32)],
        compiler_params=_params(("arbitrary",)),
        name="lru_bwd" if reverse else "lru_fwd",
    )(*args)


def _block_diag(w):
    g, bw, _ = w.shape
    eye = jnp.eye(g, dtype=w.dtype)
    return (eye[:, None, :, None] * w[:, :, None, :]).reshape(g * bw, g * bw)


def _final_kernel(x_ref, g_ref, o_ref):
    o_ref[...] = _rms(x_ref[...], g_ref[...])


def _final_norm(x, g, n):
    t, d = x.shape
    skip = (t - n) // ROW_BLOCK
    return pl.pallas_call(
        _final_kernel,
        grid=(n // ROW_BLOCK,),
        in_specs=[pl.BlockSpec((ROW_BLOCK, d), lambda i: (i + skip, 0)), pl.BlockSpec((1, d), lambda i: (0, 0))],
        out_specs=pl.BlockSpec((ROW_BLOCK, d), lambda i: (i, 0)),
        out_shape=jax.ShapeDtypeStruct((n, d), F32),
        compiler_params=_params(("parallel",)),
        name="final_norm",
    )(x, g)


def _rope_tables(n, ctx_len, rot_dim, lead, tail, repeat):
    t = jnp.arange(n)
    row = (t // GRID_W).astype(F32)
    col = (t % GRID_W).astype(F32)
    n_freq = rot_dim // 4
    inv = ROPE_BASE ** (-jnp.arange(n_freq, dtype=F32) / n_freq)
    ang = jnp.concatenate([row[:, None] * inv, col[:, None] * inv], axis=-1)
    cos, sin = jnp.cos(ang), jnp.sin(ang)
    cos_l = jnp.concatenate([jnp.ones((n, lead), F32)] + [cos, cos] * repeat + [jnp.ones((n, tail), F32)], axis=-1)
    sin_l = jnp.concatenate([jnp.zeros((n, lead), F32)] + [-sin, sin] * repeat + [jnp.zeros((n, tail), F32)], axis=-1)
    cos_t = jnp.concatenate([jnp.ones((ctx_len, LANES), F32), cos_l], axis=0)
    sin_t = jnp.concatenate([jnp.zeros((ctx_len, LANES), F32), sin_l], axis=0)
    return cos_t, sin_t


def _even_weights(w_in, w_uq, w_ukv):
    d = w_in.shape[0]
    q, k, v, g_na, lq, lkv, kr, g_mla = jnp.split(
        w_in, [512, 1024, 1536, 2048, 2048 + Q_LORA, 2048 + Q_LORA + KV_LORA, 2048 + Q_LORA + KV_LORA + MLA_ROPE], axis=1)
    half = MLA_ROPE // 2
    pad = LANES - MLA_QK
    kr_pad = jnp.concatenate([jnp.zeros((d, MLA_NOPE), F32), kr, jnp.zeros((d, pad), F32)], axis=1)
    krs_pad = jnp.concatenate([jnp.zeros((d, MLA_NOPE), F32), kr[:, half:], kr[:, :half], jnp.zeros((d, pad), F32)], axis=1)
    w = jnp.concatenate([q * (NA_HD ** -0.5), k, v, g_na, g_mla, lq, lkv, kr_pad, krs_pad], axis=1).astype(BF16)
    uq = w_uq.reshape(Q_LORA, MLA_HEADS, MLA_QK)
    zq = jnp.zeros((Q_LORA, MLA_HEADS, pad), F32)
    uq_main = jnp.concatenate([uq, zq], axis=-1)
    uq_swap = jnp.concatenate([jnp.zeros((Q_LORA, MLA_HEADS, MLA_NOPE), F32), uq[..., MLA_NOPE + half:],
                               uq[..., MLA_NOPE:MLA_NOPE + half], zq], axis=-1)
    wuq = jnp.concatenate([uq_main.reshape(Q_LORA, -1), uq_swap.reshape(Q_LORA, -1)], axis=1).astype(BF16)
    ukv = w_ukv.reshape(KV_LORA, MLA_HEADS, MLA_NOPE + MLA_V)
    k_nope = jnp.concatenate([ukv[..., :MLA_NOPE], jnp.zeros((KV_LORA, MLA_HEADS, LANES - MLA_NOPE), F32)], axis=-1)
    wkv = jnp.concatenate([k_nope.reshape(KV_LORA, -1), ukv[..., MLA_NOPE:].reshape(KV_LORA, -1)], axis=1).astype(BF16)
    return w, wuq, wkv


def _odd_weights(w_in):
    d = w_in.shape[0]
    q, k, v, g_d, u, g_lru = jnp.split(w_in, [512, 1024, 1536, 2048, 2560], axis=1)

    def swap(m):
        return m.reshape(d, DIFF_HEADS, 2, 2, DIFF_HD // 2)[:, :, :, ::-1, :].reshape(d, DIFF_W)

    scale = DIFF_HD ** -0.5
    return jnp.concatenate([q * scale, swap(q) * scale, k, swap(k), v, g_d, g_lru, u], axis=1).astype(BF16)


def kernel(x, c, ctx, c_ctx, mod_w, mod_b, norm_g, final_g, e_w_in, e_w_out, na_rpb, mla_q_norm, mla_w_uq, mla_kv_norm, mla_w_ukv, o_w_in, o_w_out, diff_lq1, diff_lk1, diff_lq2, diff_lk2, diff_subln, lru_conv_w, lru_conv_b, lru_lambda, lru_wa, lru_ba, lru_wx, lru_bx):
    batch, n, d = x.shape
    ctx_len = ctx.shape[1]
    assert batch == 1 and ctx_len == ROW_BLOCK and d == D_MODEL
    assert n // GRID_W >= WIN_R

    cvec = jnp.concatenate([c_ctx[None], c, jnp.zeros((SUBLANES - 2, d), F32)], axis=0)
    mods = _modulation(cvec, mod_w, mod_b)[:, 0:2].reshape(DEPTH, 2, 1, 3 * d)

    cos_m, sin_m = _rope_tables(n, ctx_len, MLA_ROPE, MLA_NOPE, LANES - MLA_QK, 1)
    cos_d, sin_d = _rope_tables(n, ctx_len, DIFF_HD, 0, 0, 2)

    xs = jnp.concatenate([ctx[0], x[0]], axis=0)
    for l in range(DEPTH):
        i = l // 2
        g_row = norm_g[l][None]
        if l % 2 == 0:
            w, wuq, wkv = _even_weights(e_w_in[i], mla_w_uq[i], mla_w_ukv[i])
            naq, nak, nav, gate, mq, mk, mv = _in_even(
                xs, g_row, mods[l], w, mla_q_norm[i][None], wuq, mla_kv_norm[i][None], wkv, cos_m, sin_m)
            a = _na_attention(naq, nak, nav, _na_bias_table(na_rpb[i]))
            b = _flash(mq, mk, mv, (), diff=False)
            w_out = e_w_out[i]
        else:
            lam_init = 0.8 - 0.6 * math.exp(-0.3 * l)
            dq, dk, dv, gate, u = _in_odd(xs, g_row, mods[l], _odd_weights(o_w_in[i]), cos_d, sin_d)
            lamv = jnp.stack([diff_lq1[i], diff_lk1[i], diff_lq2[i], diff_lk2[i]])
            a = _flash(dq, dk, dv, (lamv, diff_subln[i][None]), diff=True, lam_init=lam_init)
            lru = lambda dr, hf: _lru_pass(
                u, lru_conv_w[i], lru_conv_b[i][None], lru_lambda[i, dr][None],
                _block_diag(lru_wa[i, dr]).astype(BF16), lru_ba[i, dr][None],
                _block_diag(lru_wx[i, dr]).astype(BF16), lru_bx[i, dr][None], hf, reverse=dr == 1)
            b = lru(1, lru(0, None))
            w_out = o_w_out[i]
        xs = _out_proj(a, b, gate, w_out.astype(BF16), xs, mods[l])
    return _final_norm(xs, final_g[None], n)[None]
```

```python
import functools
import math

import jax
import jax.numpy as jnp
import numpy as np
from jax import lax
from jax.experimental import pallas as pl
from jax.experimental.pallas import tpu as pltpu

F32 = jnp.float32
BF16 = jnp.bfloat16

D_MODEL = 1024
DEPTH = 4
GRID_W = 64
EPS = 1e-6
ROPE_BASE = 10000.0

NA_HEADS = 8
NA_HD = 64
NA_W = NA_HEADS * NA_HD
WIN_R = 8
WIN_C = 16

MLA_HEADS = 8
MLA_NOPE = 64
MLA_ROPE = 32
MLA_V = 64
MLA_QK = MLA_NOPE + MLA_ROPE
MLA_W = MLA_HEADS * MLA_V
Q_LORA = 384
KV_LORA = 256

DIFF_HEADS = 4
DIFF_HD = 64
DIFF_W = DIFF_HEADS * 2 * DIFF_HD

LRU_W = 512
LRU_BLOCKS = 8
LRU_BW = LRU_W // LRU_BLOCKS
CONV_W = 4
LRU_C = 8.0

LANES = 128
SUBLANES = 8
ROW_BLOCK = 256
KV_CHUNK = 1024
SOFTMAX_ROWS = 256
VMEM_LIMIT = 56 * 1024 * 1024
NEG_BIG = -1e30
LOG2E = math.log2(math.e)

_NT = (((1,), (1,)), ((), ()))


def _dot_nt(a, b):
    return lax.dot_general(a, b, _NT, preferred_element_type=F32)


def _dot(a, b):
    return jnp.dot(a, b, preferred_element_type=F32)


def _params(sem, flags=None):
    return pltpu.CompilerParams(dimension_semantics=sem, vmem_limit_bytes=VMEM_LIMIT, flags=flags)


def _mod_kernel(c_ref, w_ref, b_ref, o_ref):
    c = c_ref[...]
    s = c * jax.nn.sigmoid(c)
    o_ref[...] = _dot(s.astype(BF16), w_ref[...].astype(BF16)) + b_ref[...]


def _modulation(cvec, mod_w, mod_b):
    depth, d, d3 = mod_w.shape
    tn = 512
    return pl.pallas_call(
        _mod_kernel,
        grid=(depth, d3 // tn),
        in_specs=[
            pl.BlockSpec((SUBLANES, d), lambda l, j: (0, 0)),
            pl.BlockSpec((None, d, tn), lambda l, j: (l, 0, j)),
            pl.BlockSpec((None, 1, tn), lambda l, j: (l, 0, j)),
        ],
        out_specs=pl.BlockSpec((None, SUBLANES, tn), lambda l, j: (l, 0, j)),
        out_shape=jax.ShapeDtypeStruct((depth, SUBLANES, d3), F32),
        compiler_params=_params(("parallel", "parallel")),
        name="modulation",
    )(cvec, mod_w, mod_b.reshape(depth, 1, d3))


def _norm_mod(x, g, mod):
    d = x.shape[-1]
    y = x * lax.rsqrt(jnp.mean(x * x, axis=-1, keepdims=True) + EPS) * g
    return y * (1.0 + mod[:, d:2 * d]) + mod[:, 0:d]


def _rms(x, g):
    return x * lax.rsqrt(jnp.mean(x * x, axis=-1, keepdims=True) + EPS) * g


E_Q, E_K, E_V, E_G, E_LQ, E_LKV, E_KR, E_KRS, E_END = 0, 512, 1024, 1536, 2560, 2944, 3200, 3328, 3456


def _in_even_kernel(x_ref, g_ref, mod_ref, w_ref, qn_ref, wuq_ref, kvn_ref, wkv_ref, c_ref, s_ref,
                    naq_ref, nak_ref, nav_ref, gate_ref, mq_ref, mk_ref, mv_ref):
    hb = _norm_mod(x_ref[...], g_ref[...], mod_ref[...]).astype(BF16)
    npair = NA_HEADS // 2
    for ref, off in ((naq_ref, E_Q), (nak_ref, E_K), (nav_ref, E_V)):
        seg = _dot(hb, w_ref[:, off:off + NA_W])
        for p in range(npair):
            ref[p] = seg[:, p * LANES:(p + 1) * LANES].astype(BF16)
    gate_ref[...] = _dot(hb, w_ref[:, E_G:E_LQ])
    cos = c_ref[...]
    sin = s_ref[...]
    qn = _rms(_dot(hb, w_ref[:, E_LQ:E_LKV]), qn_ref[...]).astype(BF16)
    width = MLA_HEADS * LANES
    q_main = _dot(qn, wuq_ref[:, 0:width])
    q_swap = _dot(qn, wuq_ref[:, width:2 * width])
    scale = MLA_QK ** -0.5 * LOG2E
    for h in range(MLA_HEADS):
        sl = slice(h * LANES, (h + 1) * LANES)
        mq_ref[h] = ((q_main[:, sl] * cos + q_swap[:, sl] * sin) * scale).astype(BF16)
    kvn = _rms(_dot(hb, w_ref[:, E_LKV:E_KR]), kvn_ref[...]).astype(BF16)
    kr = _dot(hb, w_ref[:, E_KR:E_KRS]) * cos + _dot(hb, w_ref[:, E_KRS:E_END]) * sin
    k_nope = _dot(kvn, wkv_ref[:, 0:width])
    for h in range(MLA_HEADS):
        mk_ref[h] = (k_nope[:, h * LANES:(h + 1) * LANES] + kr).astype(BF16)
    v = _dot(kvn, wkv_ref[:, width:width + MLA_W])
    for p in range(MLA_HEADS // 2):
        mv_ref[p] = v[:, p * LANES:(p + 1) * LANES].T.astype(BF16)


def _in_even(x, norm_g, mod, w, qn, wuq, kvn, wkv, cos, sin):
    t, d = x.shape
    nb = t // ROW_BLOCK
    full = lambda a: pl.BlockSpec(a.shape, lambda i: (0,) * a.ndim)
    rows = lambda c: pl.BlockSpec((ROW_BLOCK, c), lambda i: (i, 0))
    heads = lambda n: pl.BlockSpec((n, ROW_BLOCK, LANES), lambda i: (0, i, 0))
    hshape = lambda n: jax.ShapeDtypeStruct((n, t, LANES), BF16)
    return pl.pallas_call(
        _in_even_kernel,
        grid=(nb,),
        in_specs=[rows(d), full(norm_g),
                  pl.BlockSpec((None, 1, 3 * d), lambda i: (jnp.minimum(i, 1), 0, 0)),
                  full(w), full(qn), full(wuq), full(kvn), full(wkv), rows(LANES), rows(LANES)],
        out_specs=[heads(4), heads(4), heads(4), rows(2 * NA_W), heads(8), heads(8),
                   pl.BlockSpec((4, LANES, ROW_BLOCK), lambda i: (0, 0, i))],
        out_shape=[hshape(4), hshape(4), hshape(4), jax.ShapeDtypeStruct((t, 2 * NA_W), F32),
                   hshape(8), hshape(8), jax.ShapeDtypeStruct((4, LANES, t), BF16)],
        compiler_params=_params(("parallel",)),
        name="in_proj_even",
    )(x, norm_g, mod, w, qn, wuq, kvn, wkv, cos, sin)


O_Q, O_QS, O_K, O_KS, O_V, O_G, O_U, O_END = 0, 512, 1024, 1536, 2048, 2560, 3584, 4096


def _in_odd_kernel(x_ref, g_ref, mod_ref, w_ref, c_ref, s_ref, dq_ref, dk_ref, dv_ref, gate_ref, u_ref):
    hb = _norm_mod(x_ref[...], g_ref[...], mod_ref[...]).astype(BF16)
    cos = c_ref[...]
    sin = s_ref[...]
    lane = lax.broadcasted_iota(jnp.int32, (1, LANES), 1)
    first = lane < DIFF_HD
    q = _dot(hb, w_ref[:, O_Q:O_QS])
    qs = _dot(hb, w_ref[:, O_QS:O_K])
    k = _dot(hb, w_ref[:, O_K:O_KS])
    ks = _dot(hb, w_ref[:, O_KS:O_V])
    v = _dot(hb, w_ref[:, O_V:O_G])
    for h in range(DIFF_HEADS):
        sl = slice(h * LANES, (h + 1) * LANES)
        qr = (q[:, sl] * cos + qs[:, sl] * sin) * LOG2E
        dq_ref[h, 0] = jnp.where(first, qr, 0.0).astype(BF16)
        dq_ref[h, 1] = jnp.where(first, 0.0, qr).astype(BF16)
        dk_ref[h] = (k[:, sl] * cos + ks[:, sl] * sin).astype(BF16)
        dv_ref[h] = v[:, sl].T.astype(BF16)
    gate_ref[...] = _dot(hb, w_ref[:, O_G:O_U])
    u_ref[...] = _dot(hb, w_ref[:, O_U:O_END])


def _in_odd(x, norm_g, mod, w, cos, sin):
    t, d = x.shape
    nb = t // ROW_BLOCK
    full = lambda a: pl.BlockSpec(a.shape, lambda i: (0,) * a.ndim)
    rows = lambda c: pl.BlockSpec((ROW_BLOCK, c), lambda i: (i, 0))
    heads = pl.BlockSpec((DIFF_HEADS, ROW_BLOCK, LANES), lambda i: (0, i, 0))
    hshape = jax.ShapeDtypeStruct((DIFF_HEADS, t, LANES), BF16)
    return pl.pallas_call(
        _in_odd_kernel,
        grid=(nb,),
        in_specs=[rows(d), full(norm_g),
                  pl.BlockSpec((None, 1, 3 * d), lambda i: (jnp.minimum(i, 1), 0, 0)),
                  full(w), rows(LANES), rows(LANES)],
        out_specs=[pl.BlockSpec((DIFF_HEADS, 2, ROW_BLOCK, LANES), lambda i: (0, 0, i, 0)),
                   heads, pl.BlockSpec((DIFF_HEADS, LANES, ROW_BLOCK), lambda i: (0, 0, i)),
                   rows(DIFF_W + LRU_W), rows(LRU_W)],
        out_shape=[jax.ShapeDtypeStruct((DIFF_HEADS, 2, t, LANES), BF16), hshape,
                   jax.ShapeDtypeStruct((DIFF_HEADS, LANES, t), BF16),
                   jax.ShapeDtypeStruct((t, DIFF_W + LRU_W), F32), jax.ShapeDtypeStruct((t, LRU_W), F32)],
        compiler_params=_params(("parallel",)),
        name="in_proj_odd",
    )(x, norm_g, mod, w, cos, sin)


def _sublane_all(op, x):
    for k in (4, 2, 1):
        x = op(x, pltpu.roll(x, k, 0))
    return x


def _flash_kernel(*refs, diff, lam_init):
    if diff:
        q_ref, k_ref, vt_ref, lamv_ref, subln_ref, o_ref = refs[:6]
    else:
        q_ref, k_ref, vt_ref, o_ref = refs[:4]
    m_sc, l_sc, acc_sc, s0_sc, s1_sc, p0_sc, p1_sc, a0_sc, a1_sc, c0_sc, c1_sc = refs[-11:]
    i = pl.program_id(1)
    tq = q_ref.shape[1]
    dv = vt_ref.shape[0]
    t = vt_ref.shape[1]

    def scores(s_sc, c_sc, start, size):
        for s in range(2):
            kc = k_ref[pl.ds(start, size), :] if diff else k_ref[s, pl.ds(start, size), :]
            st = _dot_nt(kc, q_ref[s])
            s_sc[s, 0:size, :] = st
            c_sc[s] = jnp.max(st.reshape(size // SUBLANES, SUBLANES, tq), axis=0)

    def softmax(s_sc, c_sc, p_sc, a_sc, size):
        for s in range(2):
            m_prev = m_sc[s]
            m_new = jnp.maximum(m_prev, _sublane_all(jnp.maximum, c_sc[s]))
            alpha = jnp.exp2(m_prev - m_new)
            l = alpha * l_sc[s]
            for r in range(0, size, SOFTMAX_ROWS):
                x = s_sc[s, r:r + SOFTMAX_ROWS, :].reshape(SOFTMAX_ROWS // SUBLANES, SUBLANES, tq) - m_new[None]
                p = jnp.exp2(x)
                l = l + jnp.sum(p, axis=0)
                p_sc[s, r:r + SOFTMAX_ROWS, :] = p.reshape(SOFTMAX_ROWS, tq).astype(BF16)
            l_sc[s] = l
            a_sc[s] = alpha
            m_sc[s] = m_new

    def accumulate(p_sc, a_sc, start):
        vt = vt_ref[:, pl.ds(start, KV_CHUNK)]
        for s in range(2):
            acc = acc_sc[s].reshape(dv // SUBLANES, SUBLANES, tq) * a_sc[s][None]
            acc_sc[s] = acc.reshape(dv, tq) + _dot(vt, p_sc[s])

    m_sc[...] = jnp.full(m_sc.shape, NEG_BIG, F32)
    l_sc[...] = jnp.zeros(l_sc.shape, F32)
    acc_sc[...] = jnp.zeros(acc_sc.shape, F32)
    scores(s0_sc, c0_sc, 0, ROW_BLOCK)
    p0_sc[:, ROW_BLOCK:, :] = jnp.zeros((2, KV_CHUNK - ROW_BLOCK, tq), BF16)
    softmax(s0_sc, c0_sc, p0_sc, a0_sc, ROW_BLOCK)

    scores(s1_sc, c1_sc, ROW_BLOCK, KV_CHUNK)
    bufs = ((s0_sc, c0_sc, p0_sc, a0_sc), (s1_sc, c1_sc, p1_sc, a1_sc))

    n_chunks = (t - ROW_BLOCK) // KV_CHUNK

    @pl.when(i > 0)
    def _():
        for n in range(1, n_chunks + 1):
            s_a, c_a, p_a, a_a = bufs[(n - 1) % 2]
            s_b, c_b, p_b, a_b = bufs[n % 2]
            accumulate(p_a, a_a, max(ROW_BLOCK + (n - 2) * KV_CHUNK, 0))
            if n < n_chunks:
                scores(s_a, c_a, ROW_BLOCK + n * KV_CHUNK, KV_CHUNK)
            softmax(s_b, c_b, p_b, a_b, KV_CHUNK)

    accumulate(p0_sc, a0_sc, jnp.where(i == 0, 0, t - KV_CHUNK))

    def result(s):
        l = _sublane_all(jnp.add, l_sc[s])
        o_t = acc_sc[s].reshape(dv // SUBLANES, SUBLANES, tq) / l[None]
        return o_t.reshape(dv, tq).T

    o0 = result(0)
    o1 = result(1)
    if diff:
        lv = lamv_ref[...]
        lam = (jnp.exp(jnp.sum(lv[0:1] * lv[1:2], axis=-1, keepdims=True))
               - jnp.exp(jnp.sum(lv[2:3] * lv[3:4], axis=-1, keepdims=True)) + lam_init)
        o = o0 - lam * o1
        o_ref[...] = _rms(o, subln_ref[...]) * (1.0 - lam_init)
    else:
        lane = lax.broadcasted_iota(jnp.int32, (1, LANES), 1)
        o_ref[...] = jnp.where(lane < MLA_V, o0, o1)


def _flash(q, k, vt, extra, *, diff, lam_init=0.0):
    groups, t = vt.shape[0], vt.shape[2]
    nb = t // ROW_BLOCK
    assert (t - ROW_BLOCK) % (2 * KV_CHUNK) == 0
    if diff:
        q_spec = pl.BlockSpec((None, 2, ROW_BLOCK, LANES), lambda g, i: (g, 0, i, 0))
        k_spec = pl.BlockSpec((None, t, LANES), lambda g, i: (g, 0, 0))
        extra_specs = [pl.BlockSpec(e.shape, lambda g, i: (0, 0)) for e in extra]
    else:
        q_spec = pl.BlockSpec((2, ROW_BLOCK, LANES), lambda g, i: (g, i, 0))
        k_spec = pl.BlockSpec((2, t, LANES), lambda g, i: (g, 0, 0))
        extra_specs = []
    return pl.pallas_call(
        functools.partial(_flash_kernel, diff=diff, lam_init=lam_init),
        grid=(groups, nb),
        in_specs=[q_spec, k_spec, pl.BlockSpec((None, LANES, t), lambda g, i: (g, 0, 0))] + extra_specs,
        out_specs=pl.BlockSpec((ROW_BLOCK, LANES), lambda g, i: (i, g)),
        out_shape=jax.ShapeDtypeStruct((t, groups * LANES), F32),
        scratch_shapes=[pltpu.VMEM((2, SUBLANES, ROW_BLOCK), F32), pltpu.VMEM((2, SUBLANES, ROW_BLOCK), F32),
                        pltpu.VMEM((2, LANES, ROW_BLOCK), F32)]
        + [pltpu.VMEM((2, KV_CHUNK, ROW_BLOCK), F32)] * 2 + [pltpu.VMEM((2, KV_CHUNK, ROW_BLOCK), BF16)] * 2
        + [pltpu.VMEM((2, SUBLANES, ROW_BLOCK), F32)] * 4,
        compiler_params=_params(("parallel", "arbitrary")),
        name="flash_diff" if diff else "flash_mla",
    )(q, k, vt, *extra)


NA_ROWS_PER_STEP = ROW_BLOCK // GRID_W


def _na_kernel(q_ref, k_ref, v_ref, bt_ref, o_ref, *, grid_rows):
    i = pl.program_id(1)
    lane = lax.broadcasted_iota(jnp.int32, (1, LANES), 1)
    first = lane < NA_HD
    kctx = k_ref[0:ROW_BLOCK, :]
    vctx = v_ref[0:ROW_BLOCK, :]

    def stacked(q):
        zero = jnp.zeros_like(q)
        return jnp.concatenate([jnp.where(first, q, zero), jnp.where(first, zero, q)], axis=0)

    @pl.when(i == 0)
    def _():
        q2 = stacked(q_ref[...])
        sc = _dot_nt(q2, kctx)
        p = jnp.exp(sc - jnp.max(sc, axis=-1, keepdims=True))
        o = _dot(p.astype(BF16), vctx) / jnp.sum(p, axis=-1, keepdims=True)
        o_ref[...] = jnp.where(first, o[0:ROW_BLOCK], o[ROW_BLOCK:])

    @pl.when(i > 0)
    def _():
        nkeys = WIN_R * GRID_W
        for rr in range(NA_ROWS_PER_STEP):
            r = (i - 1) * NA_ROWS_PER_STEP + rr
            r0 = jnp.clip(r - WIN_R // 2, 0, grid_rows - WIN_R)
            start = pl.multiple_of(ROW_BLOCK + r0 * GRID_W, GRID_W)
            kl = k_ref[pl.ds(start, nkeys), :]
            vl = v_ref[pl.ds(start, nkeys), :]
            q2 = stacked(q_ref[rr * GRID_W:(rr + 1) * GRID_W, :])
            s_lat = _dot_nt(q2, kl) + bt_ref[r0 - r + WIN_R - 1]
            s_ctx = _dot_nt(q2, kctx)
            m = jnp.maximum(jnp.max(s_lat, axis=-1, keepdims=True), jnp.max(s_ctx, axis=-1, keepdims=True))
            p_lat = jnp.exp(s_lat - m)
            p_ctx = jnp.exp(s_ctx - m)
            l = jnp.sum(p_lat, axis=-1, keepdims=True) + jnp.sum(p_ctx, axis=-1, keepdims=True)
            o = (_dot(p_lat.astype(BF16), vl) + _dot(p_ctx.astype(BF16), vctx)) / l
            o_ref[rr * GRID_W:(rr + 1) * GRID_W, :] = jnp.where(first, o[0:GRID_W], o[GRID_W:])


def _na_attention(q, k, v, bias):
    npair, t = q.shape[0], q.shape[1]
    nb = t // ROW_BLOCK
    grid_rows = (t - ROW_BLOCK) // GRID_W
    resident = pl.BlockSpec((None, t, LANES), lambda p, i: (p, 0, 0))
    return pl.pallas_call(
        functools.partial(_na_kernel, grid_rows=grid_rows),
        grid=(npair, nb),
        in_specs=[pl.BlockSpec((None, ROW_BLOCK, LANES), lambda p, i: (p, i, 0)), resident, resident,
                  pl.BlockSpec((None,) + bias.shape[1:], lambda p, i: (p, 0, 0, 0))],
        out_specs=pl.BlockSpec((ROW_BLOCK, LANES), lambda p, i: (i, p)),
        out_shape=jax.ShapeDtypeStruct((t, npair * LANES), F32),
        compiler_params=_params(("parallel", "arbitrary")),
        name="na_attention",
    )(q, k, v, bias)


def _na_bias_table(rpb):
    w = np.arange(GRID_W)
    c0 = np.clip(w - WIN_C // 2, 0, GRID_W - WIN_C)
    inside = (w[None, :] >= c0[:, None]) & (w[None, :] < c0[:, None] + WIN_C)
    edge = GRID_W - WIN_C
    pad = jnp.pad(rpb, ((0, 0), (0, 0), (edge, edge)))
    col = jnp.stack([pad[:, :, GRID_W - 1 - wq:2 * GRID_W - 1 - wq] for wq in range(GRID_W)], axis=2)
    col = jnp.where(inside[None, None], col, NEG_BIG)
    tab = jnp.stack([col[:, var:var + WIN_R] for var in range(WIN_R)], axis=1)
    tab = tab.transpose(0, 1, 3, 2, 4).reshape(NA_HEADS // 2, 2, WIN_R, GRID_W, WIN_R * GRID_W)
    return tab.transpose(0, 2, 1, 3, 4).reshape(NA_HEADS // 2, WIN_R, 2 * GRID_W, WIN_R * GRID_W)


def _out_kernel(*refs, final):
    if final:
        a_ref, b_ref, gate_ref, w_ref, x_ref, mod_ref, fg_ref, o_ref = refs
    else:
        a_ref, b_ref, gate_ref, w_ref, x_ref, mod_ref, o_ref = refs
    half = a_ref.shape[-1]
    d = x_ref.shape[-1]
    ga = gate_ref[:, 0:half]
    gb = gate_ref[:, half:2 * half]
    ya = (a_ref[...] * (ga * jax.nn.sigmoid(ga))).astype(BF16)
    yb = (b_ref[...] * (gb * jax.nn.sigmoid(gb))).astype(BF16)
    y = _dot(ya, w_ref[0:half, :]) + _dot(yb, w_ref[half:2 * half, :])
    x_new = x_ref[...] + mod_ref[:, 2 * d:3 * d] * y
    o_ref[...] = _rms(x_new, fg_ref[...]) if final else x_new


def _out_proj(a, b, gate, w, x, mod, final_g=None):
    t, d = x.shape
    nb = t // ROW_BLOCK
    final = final_g is not None
    rows = lambda c: pl.BlockSpec((ROW_BLOCK, c), lambda i: (i, 0))
    in_specs = [rows(a.shape[1]), rows(b.shape[1]), rows(gate.shape[1]),
                pl.BlockSpec(w.shape, lambda i: (0, 0)), rows(d),
                pl.BlockSpec((None, 1, 3 * d), lambda i: (jnp.minimum(i, 1), 0, 0))]
    args = [a, b, gate, w, x, mod]
    if final:
        in_specs.append(pl.BlockSpec((1, d), lambda i: (0, 0)))
        args.append(final_g)
    return pl.pallas_call(
        functools.partial(_out_kernel, final=final),
        grid=(nb,),
        in_specs=in_specs,
        out_specs=pl.BlockSpec((ROW_BLOCK, d), lambda i: (jnp.maximum(i - 1, 0), 0)) if final else rows(d),
        out_shape=jax.ShapeDtypeStruct((t - ROW_BLOCK if final else t, d), F32),
        compiler_params=_params(("arbitrary",) if final else ("parallel",)),
        name="out_proj_final" if final else "out_proj",
    )(*args)


def _lru_kernel(*refs, reverse, nb):
    if reverse:
        (x_ref, xp_ref, xn_ref, cw_ref, cb_ref, lam_ref, wa_ref, ba_ref, wx_ref, bx_ref, hf_ref,
         o_ref, xs_sc, a_sc, b_sc, carry_sc) = refs
    else:
        (x_ref, xp_ref, xn_ref, cw_ref, cb_ref, lam_ref, wa_ref, ba_ref, wx_ref, bx_ref,
         o_ref, xs_sc, a_sc, b_sc, carry_sc) = refs
    step = pl.program_id(0)
    blk = jnp.where(step == 0, 0, nb - step) if reverse else step
    tb = x_ref.shape[0]

    @pl.when(step == 0)
    def _():
        carry_sc[...] = jnp.zeros_like(carry_sc)

    prev_ok = blk >= 2
    next_ok = (blk >= 1) & (blk < nb - 1)
    xs_sc[0:SUBLANES, :] = jnp.where(prev_ok, xp_ref[...], 0.0)
    xs_sc[SUBLANES:SUBLANES + tb, :] = x_ref[...]
    xs_sc[SUBLANES + tb:2 * SUBLANES + tb, :] = jnp.where(next_ok, xn_ref[...], 0.0)
    u = cb_ref[...]
    for j in range(CONV_W):
        off = SUBLANES + j - CONV_W // 2
        u = u + xs_sc[off:off + tb, :] * cw_ref[j:j + 1, :]

    ub = u.astype(BF16)
    r = jax.nn.sigmoid(_dot(ub, wa_ref[...]) + ba_ref[...])
    gi = jax.nn.sigmoid(_dot(ub, wx_ref[...]) + bx_ref[...])
    nl = -lam_ref[...]
    softplus = jnp.maximum(nl, 0.0) + jnp.log1p(jnp.exp(-jnp.abs(nl)))
    log_a = -LRU_C * r * softplus
    a_sc[...] = jnp.exp(log_a)
    b_sc[...] = jnp.sqrt(1.0 - jnp.exp(2.0 * log_a)) * (gi * u)

    row = lax.broadcasted_iota(jnp.int32, (SUBLANES, 1), 0)
    nchunk = tb // SUBLANES

    def chunk(c, carry):
        cc = nchunk - 1 - c if reverse else c
        start = pl.multiple_of(cc * SUBLANES, SUBLANES)
        a = a_sc[pl.ds(start, SUBLANES), :]
        b = b_sc[pl.ds(start, SUBLANES), :]
        for k in (1, 2, 4):
            shift = SUBLANES - k if reverse else k
            valid = (row < SUBLANES - k) if reverse else (row >= k)
            a_sh = pltpu.roll(a, shift, 0)
            b_sh = pltpu.roll(b, shift, 0)
            b = jnp.where(valid, a * b_sh + b, b)
            a = jnp.where(valid, a * a_sh, a)
        h = b + a * carry
        if reverse:
            o_ref[pl.ds(start, SUBLANES), :] = h + hf_ref[pl.ds(start, SUBLANES), :]
            new = h[0:1, :]
        else:
            o_ref[pl.ds(start, SUBLANES), :] = h
            new = h[SUBLANES - 1:SUBLANES, :]
        return jnp.broadcast_to(new, carry.shape)

    carry_sc[...] = lax.fori_loop(0, nchunk, chunk, carry_sc[...])


def _lru_pass(x, conv_w, conv_b, lam, wa, ba, wx, bx, hf, *, reverse):
    t, c = x.shape
    tb = ROW_BLOCK
    nb = t // tb
    per8 = tb // SUBLANES
    nb8 = t // SUBLANES
    if reverse:
        blk = lambda s: jnp.where(s == 0, 0, nb - s)
    else:
        blk = lambda s: s
    full = lambda a: pl.BlockSpec(a.shape, lambda s: (0,) * a.ndim)
    main = pl.BlockSpec((tb, c), lambda s: (blk(s), 0))
    args = [x, x, x, conv_w, conv_b, lam, wa, ba, wx, bx]
    in_specs = [main,
                pl.BlockSpec((SUBLANES, c), lambda s: (jnp.maximum(blk(s) * per8 - 1, 0), 0)),
                pl.BlockSpec((SUBLANES, c), lambda s: (jnp.minimum((blk(s) + 1) * per8, nb8 - 1), 0)),
                full(conv_w), full(conv_b), full(lam), full(wa), full(ba), full(wx), full(bx)]
    if reverse:
        args.append(hf)
        in_specs.append(main)
    return pl.pallas_call(
        functools.partial(_lru_kernel, reverse=reverse, nb=nb),
        grid=(nb,),
        in_specs=in_specs,
        out_specs=main,
        out_shape=jax.ShapeDtypeStruct((t, c), F32),
        scratch_shapes=[pltpu.VMEM((tb + 2 * SUBLANES, c), F32), pltpu.VMEM((tb, c), F32),
                        pltpu.VMEM((tb, c), F32), pltpu.VMEM((SUBLANES, c), F32)],
        compiler_params=_params(("arbitrary",)),
        name="lru_bwd" if reverse else "lru_fwd",
    )(*args)


def _block_diag(w):
    g, bw, _ = w.shape
    eye = jnp.eye(g, dtype=w.dtype)
    return (eye[:, None, :, None] * w[:, :, None, :]).reshape(g * bw, g * bw)


def _rope_tables(n, ctx_len, rot_dim, lead, tail, repeat):
    t = jnp.arange(n)
    row = (t // GRID_W).astype(F32)
    col = (t % GRID_W).astype(F32)
    n_freq = rot_dim // 4
    inv = ROPE_BASE ** (-jnp.arange(n_freq, dtype=F32) / n_freq)
    ang = jnp.concatenate([row[:, None] * inv, col[:, None] * inv], axis=-1)
    cos, sin = jnp.cos(ang), jnp.sin(ang)
    cos_l = jnp.concatenate([jnp.ones((n, lead), F32)] + [cos, cos] * repeat + [jnp.ones((n, tail), F32)], axis=-1)
    sin_l = jnp.concatenate([jnp.zeros((n, lead), F32)] + [-sin, sin] * repeat + [jnp.zeros((n, tail), F32)], axis=-1)
    cos_t = jnp.concatenate([jnp.ones((ctx_len, LANES), F32), cos_l], axis=0)
    sin_t = jnp.concatenate([jnp.zeros((ctx_len, LANES), F32), sin_l], axis=0)
    return cos_t, sin_t


def _even_weights(w_in, w_uq, w_ukv):
    d = w_in.shape[0]
    q, k, v, g_na, lq, lkv, kr, g_mla = jnp.split(
        w_in, [512, 1024, 1536, 2048, 2048 + Q_LORA, 2048 + Q_LORA + KV_LORA, 2048 + Q_LORA + KV_LORA + MLA_ROPE], axis=1)
    half = MLA_ROPE // 2
    pad = LANES - MLA_QK
    kr_pad = jnp.concatenate([jnp.zeros((d, MLA_NOPE), F32), kr, jnp.zeros((d, pad), F32)], axis=1)
    krs_pad = jnp.concatenate([jnp.zeros((d, MLA_NOPE), F32), kr[:, half:], kr[:, :half], jnp.zeros((d, pad), F32)], axis=1)
    w = jnp.concatenate([q * (NA_HD ** -0.5), k, v, g_na, g_mla, lq, lkv, kr_pad, krs_pad], axis=1).astype(BF16)
    uq = w_uq.reshape(Q_LORA, MLA_HEADS, MLA_QK)
    zq = jnp.zeros((Q_LORA, MLA_HEADS, pad), F32)
    uq_main = jnp.concatenate([uq, zq], axis=-1)
    uq_swap = jnp.concatenate([jnp.zeros((Q_LORA, MLA_HEADS, MLA_NOPE), F32), uq[..., MLA_NOPE + half:],
                               uq[..., MLA_NOPE:MLA_NOPE + half], zq], axis=-1)
    wuq = jnp.concatenate([uq_main.reshape(Q_LORA, -1), uq_swap.reshape(Q_LORA, -1)], axis=1).astype(BF16)
    ukv = w_ukv.reshape(KV_LORA, MLA_HEADS, MLA_NOPE + MLA_V)
    k_nope = jnp.concatenate([ukv[..., :MLA_NOPE], jnp.zeros((KV_LORA, MLA_HEADS, LANES - MLA_NOPE), F32)], axis=-1)
    wkv = jnp.concatenate([k_nope.reshape(KV_LORA, -1), ukv[..., MLA_NOPE:].reshape(KV_LORA, -1)], axis=1).astype(BF16)
    return w, wuq, wkv


def _odd_weights(w_in):
    d = w_in.shape[0]
    q, k, v, g_d, u, g_lru = jnp.split(w_in, [512, 1024, 1536, 2048, 2560], axis=1)

    def swap(m):
        return m.reshape(d, DIFF_HEADS, 2, 2, DIFF_HD // 2)[:, :, :, ::-1, :].reshape(d, DIFF_W)

    scale = DIFF_HD ** -0.5
    return jnp.concatenate([q * scale, swap(q) * scale, k, swap(k), v, g_d, g_lru, u], axis=1).astype(BF16)


def kernel(x, c, ctx, c_ctx, mod_w, mod_b, norm_g, final_g, e_w_in, e_w_out, na_rpb, mla_q_norm, mla_w_uq, mla_kv_norm, mla_w_ukv, o_w_in, o_w_out, diff_lq1, diff_lk1, diff_lq2, diff_lk2, diff_subln, lru_conv_w, lru_conv_b, lru_lambda, lru_wa, lru_ba, lru_wx, lru_bx):
    batch, n, d = x.shape
    ctx_len = ctx.shape[1]
    assert batch == 1 and ctx_len == ROW_BLOCK and d == D_MODEL
    assert n // GRID_W >= WIN_R

    cvec = jnp.concatenate([c_ctx[None], c, jnp.zeros((SUBLANES - 2, d), F32)], axis=0)
    mods = _modulation(cvec, mod_w, mod_b)[:, 0:2].reshape(DEPTH, 2, 1, 3 * d)

    cos_m, sin_m = _rope_tables(n, ctx_len, MLA_ROPE, MLA_NOPE, LANES - MLA_QK, 1)
    cos_d, sin_d = _rope_tables(n, ctx_len, DIFF_HD, 0, 0, 2)

    xs = jnp.concatenate([ctx[0], x[0]], axis=0)
    for l in range(DEPTH):
        i = l // 2
        g_row = norm_g[l][None]
        if l % 2 == 0:
            w, wuq, wkv = _even_weights(e_w_in[i], mla_w_uq[i], mla_w_ukv[i])
            naq, nak, nav, gate, mq, mk, mv = _in_even(
                xs, g_row, mods[l], w, mla_q_norm[i][None], wuq, mla_kv_norm[i][None], wkv, cos_m, sin_m)
            a = _na_attention(naq, nak, nav, _na_bias_table(na_rpb[i]))
            b = _flash(mq, mk, mv, (), diff=False)
            w_out = e_w_out[i]
        else:
            lam_init = 0.8 - 0.6 * math.exp(-0.3 * l)
            dq, dk, dv, gate, u = _in_odd(xs, g_row, mods[l], _odd_weights(o_w_in[i]), cos_d, sin_d)
            lamv = jnp.stack([diff_lq1[i], diff_lk1[i], diff_lq2[i], diff_lk2[i]])
            a = _flash(dq, dk, dv, (lamv, diff_subln[i][None]), diff=True, lam_init=lam_init)
            lru = lambda dr, hf: _lru_pass(
                u, lru_conv_w[i], lru_conv_b[i][None], lru_lambda[i, dr][None],
                _block_diag(lru_wa[i, dr]).astype(BF16), lru_ba[i, dr][None],
                _block_diag(lru_wx[i, dr]).astype(BF16), lru_bx[i, dr][None], hf, reverse=dr == 1)
            b = lru(1, lru(0, None))
            w_out = o_w_out[i]
        xs = _out_proj(a, b, gate, w_out.astype(BF16), xs, mods[l], final_g[None] if l == DEPTH - 1 else None)
    return xs[None]
```

```python
import functools
import math

import jax
import jax.numpy as jnp
import numpy as np
from jax import lax
from jax.experimental import pallas as pl
from jax.experimental.pallas import tpu as pltpu

F32 = jnp.float32
BF16 = jnp.bfloat16

D_MODEL = 1024
DEPTH = 4
GRID_W = 64
EPS = 1e-6
ROPE_BASE = 10000.0

NA_HEADS = 8
NA_HD = 64
NA_W = NA_HEADS * NA_HD
WIN_R = 8
WIN_C = 16

MLA_HEADS = 8
MLA_NOPE = 64
MLA_ROPE = 32
MLA_V = 64
MLA_QK = MLA_NOPE + MLA_ROPE
MLA_W = MLA_HEADS * MLA_V
Q_LORA = 384
KV_LORA = 256

DIFF_HEADS = 4
DIFF_HD = 64
DIFF_W = DIFF_HEADS * 2 * DIFF_HD

LRU_W = 512
LRU_BLOCKS = 8
LRU_BW = LRU_W // LRU_BLOCKS
CONV_W = 4
LRU_C = 8.0

LANES = 128
SUBLANES = 8
ROW_BLOCK = 256
KV_CHUNK = 1024
SOFTMAX_ROWS = 256
VT_ROWS = LANES + 16
VMEM_LIMIT = 56 * 1024 * 1024
NEG_BIG = -1e30
LOG2E = math.log2(math.e)

_NT = (((1,), (1,)), ((), ()))


def _dot_nt(a, b):
    return lax.dot_general(a, b, _NT, preferred_element_type=F32)


def _dot(a, b):
    return jnp.dot(a, b, preferred_element_type=F32)


def _params(sem, flags=None):
    return pltpu.CompilerParams(dimension_semantics=sem, vmem_limit_bytes=VMEM_LIMIT, flags=flags)


def _mod_kernel(c_ref, w_ref, b_ref, o_ref):
    c = c_ref[...]
    s = c * jax.nn.sigmoid(c)
    o_ref[...] = _dot(s.astype(BF16), w_ref[...].astype(BF16)) + b_ref[...]


def _modulation(cvec, mod_w, mod_b):
    depth, d, d3 = mod_w.shape
    tn = 512
    return pl.pallas_call(
        _mod_kernel,
        grid=(depth, d3 // tn),
        in_specs=[
            pl.BlockSpec((SUBLANES, d), lambda l, j: (0, 0)),
            pl.BlockSpec((None, d, tn), lambda l, j: (l, 0, j)),
            pl.BlockSpec((None, 1, tn), lambda l, j: (l, 0, j)),
        ],
        out_specs=pl.BlockSpec((None, SUBLANES, tn), lambda l, j: (l, 0, j)),
        out_shape=jax.ShapeDtypeStruct((depth, SUBLANES, d3), F32),
        compiler_params=_params(("parallel", "parallel")),
        name="modulation",
    )(cvec, mod_w, mod_b.reshape(depth, 1, d3))


def _norm_mod(x, g, mod):
    d = x.shape[-1]
    y = x * lax.rsqrt(jnp.mean(x * x, axis=-1, keepdims=True) + EPS) * g
    return y * (1.0 + mod[:, d:2 * d]) + mod[:, 0:d]


def _rms(x, g):
    return x * lax.rsqrt(jnp.mean(x * x, axis=-1, keepdims=True) + EPS) * g


E_Q, E_K, E_V, E_G, E_LQ, E_LKV, E_KR, E_KRS, E_END = 0, 512, 1024, 1536, 2560, 2944, 3200, 3328, 3456


def _in_even_kernel(x_ref, g_ref, mod_ref, w_ref, qn_ref, wuq_ref, kvn_ref, wkv_ref, c_ref, s_ref,
                    naq_ref, nak_ref, nav_ref, gate_ref, mq_ref, mk_ref, mv_ref):
    hb = _norm_mod(x_ref[...], g_ref[...], mod_ref[...]).astype(BF16)
    npair = NA_HEADS // 2
    for ref, off in ((naq_ref, E_Q), (nak_ref, E_K), (nav_ref, E_V)):
        seg = _dot(hb, w_ref[:, off:off + NA_W])
        for p in range(npair):
            ref[p] = seg[:, p * LANES:(p + 1) * LANES].astype(BF16)
    gate_ref[...] = _dot(hb, w_ref[:, E_G:E_LQ])
    cos = c_ref[...]
    sin = s_ref[...]
    qn = _rms(_dot(hb, w_ref[:, E_LQ:E_LKV]), qn_ref[...]).astype(BF16)
    width = MLA_HEADS * LANES
    q_main = _dot(qn, wuq_ref[:, 0:width])
    q_swap = _dot(qn, wuq_ref[:, width:2 * width])
    scale = MLA_QK ** -0.5 * LOG2E
    for h in range(MLA_HEADS):
        sl = slice(h * LANES, (h + 1) * LANES)
        mq_ref[h] = ((q_main[:, sl] * cos + q_swap[:, sl] * sin) * scale).astype(BF16)
    kvn = _rms(_dot(hb, w_ref[:, E_LKV:E_KR]), kvn_ref[...]).astype(BF16)
    kr = _dot(hb, w_ref[:, E_KR:E_KRS]) * cos + _dot(hb, w_ref[:, E_KRS:E_END]) * sin
    k_nope = _dot(kvn, wkv_ref[:, 0:width])
    for h in range(MLA_HEADS):
        mk_ref[h] = (k_nope[:, h * LANES:(h + 1) * LANES] + kr).astype(BF16)
    v = _dot(kvn, wkv_ref[:, width:width + MLA_W])
    for p in range(MLA_HEADS // 2):
        mv_ref[p, 0:LANES, :] = v[:, p * LANES:(p + 1) * LANES].T.astype(BF16)
    mv_ref[:, LANES:, :] = jnp.ones((MLA_HEADS // 2, VT_ROWS - LANES, ROW_BLOCK), BF16)


def _in_even(x, norm_g, mod, w, qn, wuq, kvn, wkv, cos, sin):
    t, d = x.shape
    nb = t // ROW_BLOCK
    full = lambda a: pl.BlockSpec(a.shape, lambda i: (0,) * a.ndim)
    rows = lambda c: pl.BlockSpec((ROW_BLOCK, c), lambda i: (i, 0))
    heads = lambda n: pl.BlockSpec((n, ROW_BLOCK, LANES), lambda i: (0, i, 0))
    hshape = lambda n: jax.ShapeDtypeStruct((n, t, LANES), BF16)
    return pl.pallas_call(
        _in_even_kernel,
        grid=(nb,),
        in_specs=[rows(d), full(norm_g),
                  pl.BlockSpec((None, 1, 3 * d), lambda i: (jnp.minimum(i, 1), 0, 0)),
                  full(w), full(qn), full(wuq), full(kvn), full(wkv), rows(LANES), rows(LANES)],
        out_specs=[heads(4), heads(4), heads(4), rows(2 * NA_W), heads(8), heads(8),
                   pl.BlockSpec((4, VT_ROWS, ROW_BLOCK), lambda i: (0, 0, i))],
        out_shape=[hshape(4), hshape(4), hshape(4), jax.ShapeDtypeStruct((t, 2 * NA_W), F32),
                   hshape(8), hshape(8), jax.ShapeDtypeStruct((4, VT_ROWS, t), BF16)],
        compiler_params=_params(("parallel",)),
        name="in_proj_even",
    )(x, norm_g, mod, w, qn, wuq, kvn, wkv, cos, sin)


O_Q, O_QS, O_K, O_KS, O_V, O_G, O_U, O_END = 0, 512, 1024, 1536, 2048, 2560, 3584, 4096


def _in_odd_kernel(x_ref, g_ref, mod_ref, w_ref, c_ref, s_ref, dq_ref, dk_ref, dv_ref, gate_ref, u_ref):
    hb = _norm_mod(x_ref[...], g_ref[...], mod_ref[...]).astype(BF16)
    cos = c_ref[...]
    sin = s_ref[...]
    lane = lax.broadcasted_iota(jnp.int32, (1, LANES), 1)
    first = lane < DIFF_HD
    q = _dot(hb, w_ref[:, O_Q:O_QS])
    qs = _dot(hb, w_ref[:, O_QS:O_K])
    k = _dot(hb, w_ref[:, O_K:O_KS])
    ks = _dot(hb, w_ref[:, O_KS:O_V])
    v = _dot(hb, w_ref[:, O_V:O_G])
    for h in range(DIFF_HEADS):
        sl = slice(h * LANES, (h + 1) * LANES)
        qr = (q[:, sl] * cos + qs[:, sl] * sin) * LOG2E
        dq_ref[h, 0] = jnp.where(first, qr, 0.0).astype(BF16)
        dq_ref[h, 1] = jnp.where(first, 0.0, qr).astype(BF16)
        dk_ref[h] = (k[:, sl] * cos + ks[:, sl] * sin).astype(BF16)
        dv_ref[h, 0:LANES, :] = v[:, sl].T.astype(BF16)
    dv_ref[:, LANES:, :] = jnp.ones((DIFF_HEADS, VT_ROWS - LANES, ROW_BLOCK), BF16)
    gate_ref[...] = _dot(hb, w_ref[:, O_G:O_U])
    u_ref[...] = _dot(hb, w_ref[:, O_U:O_END])


def _in_odd(x, norm_g, mod, w, cos, sin):
    t, d = x.shape
    nb = t // ROW_BLOCK
    full = lambda a: pl.BlockSpec(a.shape, lambda i: (0,) * a.ndim)
    rows = lambda c: pl.BlockSpec((ROW_BLOCK, c), lambda i: (i, 0))
    heads = pl.BlockSpec((DIFF_HEADS, ROW_BLOCK, LANES), lambda i: (0, i, 0))
    hshape = jax.ShapeDtypeStruct((DIFF_HEADS, t, LANES), BF16)
    return pl.pallas_call(
        _in_odd_kernel,
        grid=(nb,),
        in_specs=[rows(d), full(norm_g),
                  pl.BlockSpec((None, 1, 3 * d), lambda i: (jnp.minimum(i, 1), 0, 0)),
                  full(w), rows(LANES), rows(LANES)],
        out_specs=[pl.BlockSpec((DIFF_HEADS, 2, ROW_BLOCK, LANES), lambda i: (0, 0, i, 0)),
                   heads, pl.BlockSpec((DIFF_HEADS, VT_ROWS, ROW_BLOCK), lambda i: (0, 0, i)),
                   rows(DIFF_W + LRU_W), rows(LRU_W)],
        out_shape=[jax.ShapeDtypeStruct((DIFF_HEADS, 2, t, LANES), BF16), hshape,
                   jax.ShapeDtypeStruct((DIFF_HEADS, VT_ROWS, t), BF16),
                   jax.ShapeDtypeStruct((t, DIFF_W + LRU_W), F32), jax.ShapeDtypeStruct((t, LRU_W), F32)],
        compiler_params=_params(("parallel",)),
        name="in_proj_odd",
    )(x, norm_g, mod, w, cos, sin)


def _sublane_all(op, x):
    for k in (4, 2, 1):
        x = op(x, pltpu.roll(x, k, 0))
    return x


def _flash_kernel(*refs, diff, lam_init):
    if diff:
        q_ref, k_ref, vt_ref, lamv_ref, subln_ref, o_ref = refs[:6]
    else:
        q_ref, k_ref, vt_ref, o_ref = refs[:4]
    m_sc, acc_sc, s0_sc, s1_sc, p0_sc, p1_sc, a0_sc, a1_sc, c0_sc, c1_sc = refs[-10:]
    i = pl.program_id(1)
    tq = q_ref.shape[1]
    dv = vt_ref.shape[0]
    t = vt_ref.shape[1]

    def scores(s_sc, c_sc, start, size):
        for s in range(2):
            kc = k_ref[pl.ds(start, size), :] if diff else k_ref[s, pl.ds(start, size), :]
            st = _dot_nt(kc, q_ref[s])
            s_sc[s, 0:size, :] = st
            c_sc[s] = jnp.max(st.reshape(size // SUBLANES, SUBLANES, tq), axis=0)

    def softmax(s_sc, c_sc, p_sc, a_sc, size):
        for s in range(2):
            m_prev = m_sc[s]
            m_new = jnp.maximum(m_prev, _sublane_all(jnp.maximum, c_sc[s]))
            alpha = jnp.exp2(m_prev - m_new)
            for r in range(0, size, SOFTMAX_ROWS):
                x = s_sc[s, r:r + SOFTMAX_ROWS, :].reshape(SOFTMAX_ROWS // SUBLANES, SUBLANES, tq) - m_new[None]
                p = jnp.exp2(x)
                p_sc[s, r:r + SOFTMAX_ROWS, :] = p.reshape(SOFTMAX_ROWS, tq).astype(BF16)
            a_sc[s] = alpha
            m_sc[s] = m_new

    def accumulate(p_sc, a_sc, start):
        vt = vt_ref[:, pl.ds(start, KV_CHUNK)]
        for s in range(2):
            acc = acc_sc[s].reshape(dv // SUBLANES, SUBLANES, tq) * a_sc[s][None]
            acc_sc[s] = acc.reshape(dv, tq) + _dot(vt, p_sc[s])

    bufs = ((s0_sc, c0_sc, p0_sc, a0_sc), (s1_sc, c1_sc, p1_sc, a1_sc))
    n_chunks = (t - ROW_BLOCK) // KV_CHUNK

    def context_chunk():
        m_sc[...] = jnp.full(m_sc.shape, NEG_BIG, F32)
        acc_sc[...] = jnp.zeros(acc_sc.shape, F32)
        scores(s0_sc, c0_sc, 0, ROW_BLOCK)
        p0_sc[:, ROW_BLOCK:, :] = jnp.zeros((2, KV_CHUNK - ROW_BLOCK, tq), BF16)
        softmax(s0_sc, c0_sc, p0_sc, a0_sc, ROW_BLOCK)

    def result(s):
        l = acc_sc[s, LANES:LANES + SUBLANES, :]
        o_t = acc_sc[s, 0:LANES, :].reshape(LANES // SUBLANES, SUBLANES, tq) / l[None]
        return o_t.reshape(LANES, tq).T

    def finish(last_start):
        accumulate(p0_sc, a0_sc, last_start)
        o0 = result(0)
        o1 = result(1)
        if diff:
            lv = lamv_ref[...]
            lam = (jnp.exp(jnp.sum(lv[0:1] * lv[1:2], axis=-1, keepdims=True))
                   - jnp.exp(jnp.sum(lv[2:3] * lv[3:4], axis=-1, keepdims=True)) + lam_init)
            o = o0 - lam * o1
            o_ref[...] = _rms(o, subln_ref[...]) * (1.0 - lam_init)
        else:
            lane = lax.broadcasted_iota(jnp.int32, (1, LANES), 1)
            o_ref[...] = jnp.where(lane < MLA_V, o0, o1)

    context_chunk()
    scores(s1_sc, c1_sc, ROW_BLOCK, KV_CHUNK)

    @pl.when(i == 0)
    def _():
        finish(0)

    @pl.when(i > 0)
    def _():
        for n in range(1, n_chunks + 1):
            s_a, c_a, p_a, a_a = bufs[(n - 1) % 2]
            s_b, c_b, p_b, a_b = bufs[n % 2]
            if n < n_chunks:
                scores(s_a, c_a, ROW_BLOCK + n * KV_CHUNK, KV_CHUNK)
            accumulate(p_a, a_a, max(ROW_BLOCK + (n - 2) * KV_CHUNK, 0))
            softmax(s_b, c_b, p_b, a_b, KV_CHUNK)
        finish(t - KV_CHUNK)


def _flash(q, k, vt, extra, *, diff, lam_init=0.0):
    groups, t = vt.shape[0], vt.shape[2]
    nb = t // ROW_BLOCK
    assert (t - ROW_BLOCK) % (2 * KV_CHUNK) == 0
    if diff:
        q_spec = pl.BlockSpec((None, 2, ROW_BLOCK, LANES), lambda g, i: (g, 0, i, 0))
        k_spec = pl.BlockSpec((None, t, LANES), lambda g, i: (g, 0, 0))
        extra_specs = [pl.BlockSpec(e.shape, lambda g, i: (0, 0)) for e in extra]
    else:
        q_spec = pl.BlockSpec((2, ROW_BLOCK, LANES), lambda g, i: (g, i, 0))
        k_spec = pl.BlockSpec((2, t, LANES), lambda g, i: (g, 0, 0))
        extra_specs = []
    return pl.pallas_call(
        functools.partial(_flash_kernel, diff=diff, lam_init=lam_init),
        grid=(groups, nb),
        in_specs=[q_spec, k_spec, pl.BlockSpec((None, VT_ROWS, t), lambda g, i: (g, 0, 0))] + extra_specs,
        out_specs=pl.BlockSpec((ROW_BLOCK, LANES), lambda g, i: (i, g)),
        out_shape=jax.ShapeDtypeStruct((t, groups * LANES), F32),
        scratch_shapes=[pltpu.VMEM((2, SUBLANES, ROW_BLOCK), F32), pltpu.VMEM((2, VT_ROWS, ROW_BLOCK), F32)]
        + [pltpu.VMEM((2, KV_CHUNK, ROW_BLOCK), F32)] * 2 + [pltpu.VMEM((2, KV_CHUNK, ROW_BLOCK), BF16)] * 2
        + [pltpu.VMEM((2, SUBLANES, ROW_BLOCK), F32)] * 4,
        compiler_params=_params(("parallel", "arbitrary")),
        name="flash_diff" if diff else "flash_mla",
    )(q, k, vt, *extra)


NA_ROWS_PER_STEP = ROW_BLOCK // GRID_W


def _na_kernel(q_ref, k_ref, v_ref, bt_ref, o_ref, *, grid_rows):
    i = pl.program_id(1)
    lane = lax.broadcasted_iota(jnp.int32, (1, LANES), 1)
    first = lane < NA_HD
    kctx = k_ref[0:ROW_BLOCK, :]
    vctx = v_ref[0:ROW_BLOCK, :]

    def stacked(q):
        zero = jnp.zeros_like(q)
        return jnp.concatenate([jnp.where(first, q, zero), jnp.where(first, zero, q)], axis=0)

    @pl.when(i == 0)
    def _():
        q2 = stacked(q_ref[...])
        sc = _dot_nt(q2, kctx)
        p = jnp.exp(sc - jnp.max(sc, axis=-1, keepdims=True))
        o = _dot(p.astype(BF16), vctx) / jnp.sum(p, axis=-1, keepdims=True)
        o_ref[...] = jnp.where(first, o[0:ROW_BLOCK], o[ROW_BLOCK:])

    @pl.when(i > 0)
    def _():
        nkeys = WIN_R * GRID_W
        for rr in range(NA_ROWS_PER_STEP):
            r = (i - 1) * NA_ROWS_PER_STEP + rr
            r0 = jnp.clip(r - WIN_R // 2, 0, grid_rows - WIN_R)
            start = pl.multiple_of(ROW_BLOCK + r0 * GRID_W, GRID_W)
            kl = k_ref[pl.ds(start, nkeys), :]
            vl = v_ref[pl.ds(start, nkeys), :]
            q2 = stacked(q_ref[rr * GRID_W:(rr + 1) * GRID_W, :])
            s_lat = _dot_nt(q2, kl) + bt_ref[r0 - r + WIN_R - 1]
            s_ctx = _dot_nt(q2, kctx)
            m = jnp.maximum(jnp.max(s_lat, axis=-1, keepdims=True), jnp.max(s_ctx, axis=-1, keepdims=True))
            p_lat = jnp.exp(s_lat - m)
            p_ctx = jnp.exp(s_ctx - m)
            l = jnp.sum(p_lat, axis=-1, keepdims=True) + jnp.sum(p_ctx, axis=-1, keepdims=True)
            o = (_dot(p_lat.astype(BF16), vl) + _dot(p_ctx.astype(BF16), vctx)) / l
            o_ref[rr * GRID_W:(rr + 1) * GRID_W, :] = jnp.where(first, o[0:GRID_W], o[GRID_W:])


def _na_attention(q, k, v, bias):
    npair, t = q.shape[0], q.shape[1]
    nb = t // ROW_BLOCK
    grid_rows = (t - ROW_BLOCK) // GRID_W
    resident = pl.BlockSpec((None, t, LANES), lambda p, i: (p, 0, 0))
    return pl.pallas_call(
        functools.partial(_na_kernel, grid_rows=grid_rows),
        grid=(npair, nb),
        in_specs=[pl.BlockSpec((None, ROW_BLOCK, LANES), lambda p, i: (p, i, 0)), resident, resident,
                  pl.BlockSpec((None,) + bias.shape[1:], lambda p, i: (p, 0, 0, 0))],
        out_specs=pl.BlockSpec((ROW_BLOCK, LANES), lambda p, i: (i, p)),
        out_shape=jax.ShapeDtypeStruct((t, npair * LANES), F32),
        compiler_params=_params(("parallel", "arbitrary")),
        name="na_attention",
    )(q, k, v, bias)


def _na_bias_table(rpb):
    w = np.arange(GRID_W)
    c0 = np.clip(w - WIN_C // 2, 0, GRID_W - WIN_C)
    inside = (w[None, :] >= c0[:, None]) & (w[None, :] < c0[:, None] + WIN_C)
    edge = GRID_W - WIN_C
    pad = jnp.pad(rpb, ((0, 0), (0, 0), (edge, edge)))
    col = jnp.stack([pad[:, :, GRID_W - 1 - wq:2 * GRID_W - 1 - wq] for wq in range(GRID_W)], axis=2)
    col = jnp.where(inside[None, None], col, NEG_BIG)
    tab = jnp.stack([col[:, var:var + WIN_R] for var in range(WIN_R)], axis=1)
    tab = tab.transpose(0, 1, 3, 2, 4).reshape(NA_HEADS // 2, 2, WIN_R, GRID_W, WIN_R * GRID_W)
    return tab.transpose(0, 2, 1, 3, 4).reshape(NA_HEADS // 2, WIN_R, 2 * GRID_W, WIN_R * GRID_W)


def _out_kernel(*refs, final):
    if final:
        a_ref, b_ref, gate_ref, w_ref, x_ref, mod_ref, fg_ref, o_ref = refs
    else:
        a_ref, b_ref, gate_ref, w_ref, x_ref, mod_ref, o_ref = refs
    half = a_ref.shape[-1]
    d = x_ref.shape[-1]
    ga = gate_ref[:, 0:half]
    gb = gate_ref[:, half:2 * half]
    ya = (a_ref[...] * (ga * jax.nn.sigmoid(ga))).astype(BF16)
    yb = (b_ref[...] * (gb * jax.nn.sigmoid(gb))).astype(BF16)
    y = _dot(ya, w_ref[0:half, :]) + _dot(yb, w_ref[half:2 * half, :])
    x_new = x_ref[...] + mod_ref[:, 2 * d:3 * d] * y
    o_ref[...] = _rms(x_new, fg_ref[...]) if final else x_new


def _out_proj(a, b, gate, w, x, mod, final_g=None):
    t, d = x.shape
    nb = t // ROW_BLOCK
    final = final_g is not None
    rows = lambda c: pl.BlockSpec((ROW_BLOCK, c), lambda i: (i, 0))
    in_specs = [rows(a.shape[1]), rows(b.shape[1]), rows(gate.shape[1]),
                pl.BlockSpec(w.shape, lambda i: (0, 0)), rows(d),
                pl.BlockSpec((None, 1, 3 * d), lambda i: (jnp.minimum(i, 1), 0, 0))]
    args = [a, b, gate, w, x, mod]
    if final:
        in_specs.append(pl.BlockSpec((1, d), lambda i: (0, 0)))
        args.append(final_g)
    return pl.pallas_call(
        functools.partial(_out_kernel, final=final),
        grid=(nb,),
        in_specs=in_specs,
        out_specs=pl.BlockSpec((ROW_BLOCK, d), lambda i: (jnp.maximum(i - 1, 0), 0)) if final else rows(d),
        out_shape=jax.ShapeDtypeStruct((t - ROW_BLOCK if final else t, d), F32),
        compiler_params=_params(("arbitrary",) if final else ("parallel",)),
        name="out_proj_final" if final else "out_proj",
    )(*args)


def _lru_kernel(*refs, reverse, nb):
    if reverse:
        (x_ref, xp_ref, xn_ref, cw_ref, cb_ref, lam_ref, wa_ref, ba_ref, wx_ref, bx_ref, hf_ref,
         o_ref, xs_sc, a_sc, b_sc, carry_sc) = refs
    else:
        (x_ref, xp_ref, xn_ref, cw_ref, cb_ref, lam_ref, wa_ref, ba_ref, wx_ref, bx_ref,
         o_ref, xs_sc, a_sc, b_sc, carry_sc) = refs
    step = pl.program_id(0)
    blk = jnp.where(step == 0, 0, nb - step) if reverse else step
    tb = x_ref.shape[0]

    @pl.when(step == 0)
    def _():
        carry_sc[...] = jnp.zeros_like(carry_sc)

    prev_ok = blk >= 2
    next_ok = (blk >= 1) & (blk < nb - 1)
    xs_sc[0:SUBLANES, :] = jnp.where(prev_ok, xp_ref[...], 0.0)
    xs_sc[SUBLANES:SUBLANES + tb, :] = x_ref[...]
    xs_sc[SUBLANES + tb:2 * SUBLANES + tb, :] = jnp.where(next_ok, xn_ref[...], 0.0)
    u = cb_ref[...]
    for j in range(CONV_W):
        off = SUBLANES + j - CONV_W // 2
        u = u + xs_sc[off:off + tb, :] * cw_ref[j:j + 1, :]

    ub = u.astype(BF16)
    r = jax.nn.sigmoid(_dot(ub, wa_ref[...]) + ba_ref[...])
    gi = jax.nn.sigmoid(_dot(ub, wx_ref[...]) + bx_ref[...])
    nl = -lam_ref[...]
    softplus = jnp.maximum(nl, 0.0) + jnp.log1p(jnp.exp(-jnp.abs(nl)))
    log_a = -LRU_C * r * softplus
    a_sc[...] = jnp.exp(log_a)
    b_sc[...] = jnp.sqrt(1.0 - jnp.exp(2.0 * log_a)) * (gi * u)

    row = lax.broadcasted_iota(jnp.int32, (SUBLANES, 1), 0)
    nchunk = tb // SUBLANES

    def chunk(c, carry):
        cc = nchunk - 1 - c if reverse else c
        start = pl.multiple_of(cc * SUBLANES, SUBLANES)
        a = a_sc[pl.ds(start, SUBLANES), :]
        b = b_sc[pl.ds(start, SUBLANES), :]
        for k in (1, 2, 4):
            shift = SUBLANES - k if reverse else k
            valid = (row < SUBLANES - k) if reverse else (row >= k)
            a_sh = pltpu.roll(a, shift, 0)
            b_sh = pltpu.roll(b, shift, 0)
            b = jnp.where(valid, a * b_sh + b, b)
            a = jnp.where(valid, a * a_sh, a)
        h = b + a * carry
        if reverse:
            o_ref[pl.ds(start, SUBLANES), :] = h + hf_ref[pl.ds(start, SUBLANES), :]
            new = h[0:1, :]
        else:
            o_ref[pl.ds(start, SUBLANES), :] = h
            new = h[SUBLANES - 1:SUBLANES, :]
        return jnp.broadcast_to(new, carry.shape)

    carry_sc[...] = lax.fori_loop(0, nchunk, chunk, carry_sc[...])


def _lru_pass(x, conv_w, conv_b, lam, wa, ba, wx, bx, hf, *, reverse):
    t, c = x.shape
    tb = ROW_BLOCK
    nb = t // tb
    per8 = tb // SUBLANES
    nb8 = t // SUBLANES
    if reverse:
        blk = lambda s: jnp.where(s == 0, 0, nb - s)
    else:
        blk = lambda s: s
    full = lambda a: pl.BlockSpec(a.shape, lambda s: (0,) * a.ndim)
    main = pl.BlockSpec((tb, c), lambda s: (blk(s), 0))
    args = [x, x, x, conv_w, conv_b, lam, wa, ba, wx, bx]
    in_specs = [main,
                pl.BlockSpec((SUBLANES, c), lambda s: (jnp.maximum(blk(s) * per8 - 1, 0), 0)),
                pl.BlockSpec((SUBLANES, c), lambda s: (jnp.minimum((blk(s) + 1) * per8, nb8 - 1), 0)),
                full(conv_w), full(conv_b), full(lam), full(wa), full(ba), full(wx), full(bx)]
    if reverse:
        args.append(hf)
        in_specs.append(main)
    return pl.pallas_call(
        functools.partial(_lru_kernel, reverse=reverse, nb=nb),
        grid=(nb,),
        in_specs=in_specs,
        out_specs=main,
        out_shape=jax.ShapeDtypeStruct((t, c), F32),
        scratch_shapes=[pltpu.VMEM((tb + 2 * SUBLANES, c), F32), pltpu.VMEM((tb, c), F32),
                        pltpu.VMEM((tb, c), F32), pltpu.VMEM((SUBLANES, c), F32)],
        compiler_params=_params(("arbitrary",)),
        name="lru_bwd" if reverse else "lru_fwd",
    )(*args)


def _block_diag(w):
    g, bw, _ = w.shape
    eye = jnp.eye(g, dtype=w.dtype)
    return (eye[:, None, :, None] * w[:, :, None, :]).reshape(g * bw, g * bw)


def _rope_tables(n, ctx_len, rot_dim, lead, tail, repeat):
    t = jnp.arange(n)
    row = (t // GRID_W).astype(F32)
    col = (t % GRID_W).astype(F32)
    n_freq = rot_dim // 4
    inv = ROPE_BASE ** (-jnp.arange(n_freq, dtype=F32) / n_freq)
    ang = jnp.concatenate([row[:, None] * inv, col[:, None] * inv], axis=-1)
    cos, sin = jnp.cos(ang), jnp.sin(ang)
    cos_l = jnp.concatenate([jnp.ones((n, lead), F32)] + [cos, cos] * repeat + [jnp.ones((n, tail), F32)], axis=-1)
    sin_l = jnp.concatenate([jnp.zeros((n, lead), F32)] + [-sin, sin] * repeat + [jnp.zeros((n, tail), F32)], axis=-1)
    cos_t = jnp.concatenate([jnp.ones((ctx_len, LANES), F32), cos_l], axis=0)
    sin_t = jnp.concatenate([jnp.zeros((ctx_len, LANES), F32), sin_l], axis=0)
    return cos_t, sin_t


def _even_weights(w_in, w_uq, w_ukv):
    d = w_in.shape[0]
    q, k, v, g_na, lq, lkv, kr, g_mla = jnp.split(
        w_in, [512, 1024, 1536, 2048, 2048 + Q_LORA, 2048 + Q_LORA + KV_LORA, 2048 + Q_LORA + KV_LORA + MLA_ROPE], axis=1)
    half = MLA_ROPE // 2
    pad = LANES - MLA_QK
    kr_pad = jnp.concatenate([jnp.zeros((d, MLA_NOPE), F32), kr, jnp.zeros((d, pad), F32)], axis=1)
    krs_pad = jnp.concatenate([jnp.zeros((d, MLA_NOPE), F32), kr[:, half:], kr[:, :half], jnp.zeros((d, pad), F32)], axis=1)
    w = jnp.concatenate([q * (NA_HD ** -0.5), k, v, g_na, g_mla, lq, lkv, kr_pad, krs_pad], axis=1).astype(BF16)
    uq = w_uq.reshape(Q_LORA, MLA_HEADS, MLA_QK)
    zq = jnp.zeros((Q_LORA, MLA_HEADS, pad), F32)
    uq_main = jnp.concatenate([uq, zq], axis=-1)
    uq_swap = jnp.concatenate([jnp.zeros((Q_LORA, MLA_HEADS, MLA_NOPE), F32), uq[..., MLA_NOPE + half:],
                               uq[..., MLA_NOPE:MLA_NOPE + half], zq], axis=-1)
    wuq = jnp.concatenate([uq_main.reshape(Q_LORA, -1), uq_swap.reshape(Q_LORA, -1)], axis=1).astype(BF16)
    ukv = w_ukv.reshape(KV_LORA, MLA_HEADS, MLA_NOPE + MLA_V)
    k_nope = jnp.concatenate([ukv[..., :MLA_NOPE], jnp.zeros((KV_LORA, MLA_HEADS, LANES - MLA_NOPE), F32)], axis=-1)
    wkv = jnp.concatenate([k_nope.reshape(KV_LORA, -1), ukv[..., MLA_NOPE:].reshape(KV_LORA, -1)], axis=1).astype(BF16)
    return w, wuq, wkv


def _odd_weights(w_in):
    d = w_in.shape[0]
    q, k, v, g_d, u, g_lru = jnp.split(w_in, [512, 1024, 1536, 2048, 2560], axis=1)

    def swap(m):
        return m.reshape(d, DIFF_HEADS, 2, 2, DIFF_HD // 2)[:, :, :, ::-1, :].reshape(d, DIFF_W)

    scale = DIFF_HD ** -0.5
    return jnp.concatenate([q * scale, swap(q) * scale, k, swap(k), v, g_d, g_lru, u], axis=1).astype(BF16)


def kernel(x, c, ctx, c_ctx, mod_w, mod_b, norm_g, final_g, e_w_in, e_w_out, na_rpb, mla_q_norm, mla_w_uq, mla_kv_norm, mla_w_ukv, o_w_in, o_w_out, diff_lq1, diff_lk1, diff_lq2, diff_lk2, diff_subln, lru_conv_w, lru_conv_b, lru_lambda, lru_wa, lru_ba, lru_wx, lru_bx):
    batch, n, d = x.shape
    ctx_len = ctx.shape[1]
    assert batch == 1 and ctx_len == ROW_BLOCK and d == D_MODEL
    assert n // GRID_W >= WIN_R

    cvec = jnp.concatenate([c_ctx[None], c, jnp.zeros((SUBLANES - 2, d), F32)], axis=0)
    mods = _modulation(cvec, mod_w, mod_b)[:, 0:2].reshape(DEPTH, 2, 1, 3 * d)

    cos_m, sin_m = _rope_tables(n, ctx_len, MLA_ROPE, MLA_NOPE, LANES - MLA_QK, 1)
    cos_d, sin_d = _rope_tables(n, ctx_len, DIFF_HD, 0, 0, 2)

    xs = jnp.concatenate([ctx[0], x[0]], axis=0)
    for l in range(DEPTH):
        i = l // 2
        g_row = norm_g[l][None]
        if l % 2 == 0:
            w, wuq, wkv = _even_weights(e_w_in[i], mla_w_uq[i], mla_w_ukv[i])
            naq, nak, nav, gate, mq, mk, mv = _in_even(
                xs, g_row, mods[l], w, mla_q_norm[i][None], wuq, mla_kv_norm[i][None], wkv, cos_m, sin_m)
            a = _na_attention(naq, nak, nav, _na_bias_table(na_rpb[i]))
            b = _flash(mq, mk, mv, (), diff=False)
            w_out = e_w_out[i]
        else:
            lam_init = 0.8 - 0.6 * math.exp(-0.3 * l)
            dq, dk, dv, gate, u = _in_odd(xs, g_row, mods[l], _odd_weights(o_w_in[i]), cos_d, sin_d)
            lamv = jnp.stack([diff_lq1[i], diff_lk1[i], diff_lq2[i], diff_lk2[i]])
            a = _flash(dq, dk, dv, (lamv, diff_subln[i][None]), diff=True, lam_init=lam_init)
            lru = lambda dr, hf: _lru_pass(
                u, lru_conv_w[i], lru_conv_b[i][None], lru_lambda[i, dr][None],
                _block_diag(lru_wa[i, dr]).astype(BF16), lru_ba[i, dr][None],
                _block_diag(lru_wx[i, dr]).astype(BF16), lru_bx[i, dr][None], hf, reverse=dr == 1)
            b = lru(1, lru(0, None))
            w_out = o_w_out[i]
        xs = _out_proj(a, b, gate, w_out.astype(BF16), xs, mods[l], final_g[None] if l == DEPTH - 1 else None)
    return xs[None]
```

```python
import functools
import math

import jax
import jax.numpy as jnp
import numpy as np
from jax import lax
from jax.experimental import pallas as pl
from jax.experimental.pallas import tpu as pltpu

F32 = jnp.float32
BF16 = jnp.bfloat16

D_MODEL = 1024
DEPTH = 4
GRID_W = 64
EPS = 1e-6
ROPE_BASE = 10000.0

NA_HEADS = 8
NA_HD = 64
NA_W = NA_HEADS * NA_HD
WIN_R = 8
WIN_C = 16

MLA_HEADS = 8
MLA_NOPE = 64
MLA_ROPE = 32
MLA_V = 64
MLA_QK = MLA_NOPE + MLA_ROPE
MLA_W = MLA_HEADS * MLA_V
Q_LORA = 384
KV_LORA = 256

DIFF_HEADS = 4
DIFF_HD = 64
DIFF_W = DIFF_HEADS * 2 * DIFF_HD

LRU_W = 512
LRU_BLOCKS = 8
LRU_BW = LRU_W // LRU_BLOCKS
CONV_W = 4
LRU_C = 8.0

LANES = 128
SUBLANES = 8
ROW_BLOCK = 256
KV_CHUNK = 1024
SOFTMAX_ROWS = 256
VT_ROWS = LANES + 16
VMEM_LIMIT = 56 * 1024 * 1024
NEG_BIG = -1e30
LOG2E = math.log2(math.e)

_NT = (((1,), (1,)), ((), ()))


def _dot_nt(a, b):
    return lax.dot_general(a, b, _NT, preferred_element_type=F32)


def _dot(a, b):
    return jnp.dot(a, b, preferred_element_type=F32)


def _params(sem, flags=None):
    return pltpu.CompilerParams(dimension_semantics=sem, vmem_limit_bytes=VMEM_LIMIT, flags=flags)


def _mod_kernel(c_ref, w_ref, b_ref, o_ref):
    c = c_ref[...]
    s = c * jax.nn.sigmoid(c)
    o_ref[...] = _dot(s.astype(BF16), w_ref[...].astype(BF16)) + b_ref[...]


def _modulation(cvec, mod_w, mod_b):
    depth, d, d3 = mod_w.shape
    tn = 512
    return pl.pallas_call(
        _mod_kernel,
        grid=(depth, d3 // tn),
        in_specs=[
            pl.BlockSpec((SUBLANES, d), lambda l, j: (0, 0)),
            pl.BlockSpec((None, d, tn), lambda l, j: (l, 0, j)),
            pl.BlockSpec((None, 1, tn), lambda l, j: (l, 0, j)),
        ],
        out_specs=pl.BlockSpec((None, SUBLANES, tn), lambda l, j: (l, 0, j)),
        out_shape=jax.ShapeDtypeStruct((depth, SUBLANES, d3), F32),
        compiler_params=_params(("parallel", "parallel")),
        name="modulation",
    )(cvec, mod_w, mod_b.reshape(depth, 1, d3))


def _norm_mod(x, g, mod):
    d = x.shape[-1]
    y = x * lax.rsqrt(jnp.mean(x * x, axis=-1, keepdims=True) + EPS) * g
    return y * (1.0 + mod[:, d:2 * d]) + mod[:, 0:d]


def _rms(x, g):
    return x * lax.rsqrt(jnp.mean(x * x, axis=-1, keepdims=True) + EPS) * g


E_Q, E_K, E_V, E_G, E_LQ, E_LKV, E_KR, E_KRS, E_END = 0, 512, 1024, 1536, 2560, 2944, 3200, 3328, 3456


def _in_even_kernel(x_ref, g_ref, mod_ref, w_ref, qn_ref, wuq_ref, kvn_ref, wkv_ref, c_ref, s_ref,
                    naq_ref, nak_ref, nav_ref, gate_ref, mq_ref, mk_ref, mv_ref):
    hb = _norm_mod(x_ref[...], g_ref[...], mod_ref[...]).astype(BF16)
    npair = NA_HEADS // 2
    for ref, off in ((naq_ref, E_Q), (nak_ref, E_K), (nav_ref, E_V)):
        seg = _dot(hb, w_ref[:, off:off + NA_W])
        for p in range(npair):
            ref[p] = seg[:, p * LANES:(p + 1) * LANES].astype(BF16)
    gate_ref[...] = _dot(hb, w_ref[:, E_G:E_LQ])
    cos = c_ref[...]
    sin = s_ref[...]
    qn = _rms(_dot(hb, w_ref[:, E_LQ:E_LKV]), qn_ref[...]).astype(BF16)
    width = MLA_HEADS * LANES
    q_main = _dot(qn, wuq_ref[:, 0:width])
    q_swap = _dot(qn, wuq_ref[:, width:2 * width])
    scale = MLA_QK ** -0.5 * LOG2E
    for h in range(MLA_HEADS):
        sl = slice(h * LANES, (h + 1) * LANES)
        mq_ref[h] = ((q_main[:, sl] * cos + q_swap[:, sl] * sin) * scale).astype(BF16)
    kvn = _rms(_dot(hb, w_ref[:, E_LKV:E_KR]), kvn_ref[...]).astype(BF16)
    kr = _dot(hb, w_ref[:, E_KR:E_KRS]) * cos + _dot(hb, w_ref[:, E_KRS:E_END]) * sin
    k_nope = _dot(kvn, wkv_ref[:, 0:width])
    for h in range(MLA_HEADS):
        mk_ref[h] = (k_nope[:, h * LANES:(h + 1) * LANES] + kr).astype(BF16)
    v = _dot(kvn, wkv_ref[:, width:width + MLA_W])
    for p in range(MLA_HEADS // 2):
        mv_ref[p, 0:LANES, :] = v[:, p * LANES:(p + 1) * LANES].T.astype(BF16)
    mv_ref[:, LANES:, :] = jnp.ones((MLA_HEADS // 2, VT_ROWS - LANES, ROW_BLOCK), BF16)


def _in_even(x, norm_g, mod, w, qn, wuq, kvn, wkv, cos, sin):
    t, d = x.shape
    nb = t // ROW_BLOCK
    full = lambda a: pl.BlockSpec(a.shape, lambda i: (0,) * a.ndim)
    rows = lambda c: pl.BlockSpec((ROW_BLOCK, c), lambda i: (i, 0))
    heads = lambda n: pl.BlockSpec((n, ROW_BLOCK, LANES), lambda i: (0, i, 0))
    hshape = lambda n: jax.ShapeDtypeStruct((n, t, LANES), BF16)
    return pl.pallas_call(
        _in_even_kernel,
        grid=(nb,),
        in_specs=[rows(d), full(norm_g),
                  pl.BlockSpec((None, 1, 3 * d), lambda i: (jnp.minimum(i, 1), 0, 0)),
                  full(w), full(qn), full(wuq), full(kvn), full(wkv), rows(LANES), rows(LANES)],
        out_specs=[heads(4), heads(4), heads(4), rows(2 * NA_W), heads(8), heads(8),
                   pl.BlockSpec((4, VT_ROWS, ROW_BLOCK), lambda i: (0, 0, i))],
        out_shape=[hshape(4), hshape(4), hshape(4), jax.ShapeDtypeStruct((t, 2 * NA_W), F32),
                   hshape(8), hshape(8), jax.ShapeDtypeStruct((4, VT_ROWS, t), BF16)],
        compiler_params=_params(("parallel",)),
        name="in_proj_even",
    )(x, norm_g, mod, w, qn, wuq, kvn, wkv, cos, sin)


O_Q, O_QS, O_K, O_KS, O_V, O_G, O_U, O_END = 0, 512, 1024, 1536, 2048, 2560, 3584, 4096


def _in_odd_kernel(x_ref, g_ref, mod_ref, w_ref, c_ref, s_ref, dq_ref, dk_ref, dv_ref, gate_ref, u_ref):
    hb = _norm_mod(x_ref[...], g_ref[...], mod_ref[...]).astype(BF16)
    cos = c_ref[...]
    sin = s_ref[...]
    lane = lax.broadcasted_iota(jnp.int32, (1, LANES), 1)
    first = lane < DIFF_HD
    q = _dot(hb, w_ref[:, O_Q:O_QS])
    qs = _dot(hb, w_ref[:, O_QS:O_K])
    k = _dot(hb, w_ref[:, O_K:O_KS])
    ks = _dot(hb, w_ref[:, O_KS:O_V])
    v = _dot(hb, w_ref[:, O_V:O_G])
    for h in range(DIFF_HEADS):
        sl = slice(h * LANES, (h + 1) * LANES)
        qr = (q[:, sl] * cos + qs[:, sl] * sin) * LOG2E
        dq_ref[h, 0] = jnp.where(first, qr, 0.0).astype(BF16)
        dq_ref[h, 1] = jnp.where(first, 0.0, qr).astype(BF16)
        dk_ref[h] = (k[:, sl] * cos + ks[:, sl] * sin).astype(BF16)
        dv_ref[h, 0:LANES, :] = v[:, sl].T.astype(BF16)
    dv_ref[:, LANES:, :] = jnp.ones((DIFF_HEADS, VT_ROWS - LANES, ROW_BLOCK), BF16)
    gate_ref[...] = _dot(hb, w_ref[:, O_G:O_U])
    u_ref[...] = _dot(hb, w_ref[:, O_U:O_END])


def _in_odd(x, norm_g, mod, w, cos, sin):
    t, d = x.shape
    nb = t // ROW_BLOCK
    full = lambda a: pl.BlockSpec(a.shape, lambda i: (0,) * a.ndim)
    rows = lambda c: pl.BlockSpec((ROW_BLOCK, c), lambda i: (i, 0))
    heads = pl.BlockSpec((DIFF_HEADS, ROW_BLOCK, LANES), lambda i: (0, i, 0))
    hshape = jax.ShapeDtypeStruct((DIFF_HEADS, t, LANES), BF16)
    return pl.pallas_call(
        _in_odd_kernel,
        grid=(nb,),
        in_specs=[rows(d), full(norm_g),
                  pl.BlockSpec((None, 1, 3 * d), lambda i: (jnp.minimum(i, 1), 0, 0)),
                  full(w), rows(LANES), rows(LANES)],
        out_specs=[pl.BlockSpec((DIFF_HEADS, 2, ROW_BLOCK, LANES), lambda i: (0, 0, i, 0)),
                   heads, pl.BlockSpec((DIFF_HEADS, VT_ROWS, ROW_BLOCK), lambda i: (0, 0, i)),
                   rows(DIFF_W + LRU_W), rows(LRU_W)],
        out_shape=[jax.ShapeDtypeStruct((DIFF_HEADS, 2, t, LANES), BF16), hshape,
                   jax.ShapeDtypeStruct((DIFF_HEADS, VT_ROWS, t), BF16),
                   jax.ShapeDtypeStruct((t, DIFF_W + LRU_W), F32), jax.ShapeDtypeStruct((t, LRU_W), F32)],
        compiler_params=_params(("parallel",)),
        name="in_proj_odd",
    )(x, norm_g, mod, w, cos, sin)


def _sublane_all(op, x):
    for k in (4, 2, 1):
        x = op(x, pltpu.roll(x, k, 0))
    return x


def _flash_kernel(*refs, diff, lam_init):
    if diff:
        q_ref, k_ref, vt_ref, lamv_ref, subln_ref, o_ref = refs[:6]
    else:
        q_ref, k_ref, vt_ref, o_ref = refs[:4]
    m_sc, acc_sc, s0_sc, s1_sc, p0_sc, p1_sc, a0_sc, a1_sc, c0_sc, c1_sc = refs[-10:]
    i = pl.program_id(1)
    tq = q_ref.shape[1]
    dv = vt_ref.shape[0]
    t = vt_ref.shape[1]

    def scores(s_sc, c_sc, start, size):
        for s in range(2):
            kc = k_ref[pl.ds(start, size), :] if diff else k_ref[s, pl.ds(start, size), :]
            st = _dot_nt(kc, q_ref[s])
            s_sc[s, 0:size, :] = st
            c_sc[s] = jnp.max(st.reshape(size // SUBLANES, SUBLANES, tq), axis=0)

    def softmax(s_sc, c_sc, p_sc, a_sc, size):
        for s in range(2):
            m_prev = m_sc[s]
            m_new = jnp.maximum(m_prev, _sublane_all(jnp.maximum, c_sc[s]))
            alpha = jnp.exp2(m_prev - m_new)
            for r in range(0, size, SOFTMAX_ROWS):
                x = s_sc[s, r:r + SOFTMAX_ROWS, :].reshape(SOFTMAX_ROWS // SUBLANES, SUBLANES, tq) - m_new[None]
                p = jnp.exp2(x)
                p_sc[s, r:r + SOFTMAX_ROWS, :] = p.reshape(SOFTMAX_ROWS, tq).astype(BF16)
            a_sc[s] = alpha
            m_sc[s] = m_new

    def accumulate(p_sc, a_sc, start, size):
        vt = vt_ref[:, pl.ds(start, size)]
        for s in range(2):
            acc = acc_sc[s].reshape(dv // SUBLANES, SUBLANES, tq) * a_sc[s][None]
            acc_sc[s] = acc.reshape(dv, tq) + _dot(vt, p_sc[s, 0:size, :])

    bufs = ((s0_sc, c0_sc, p0_sc, a0_sc), (s1_sc, c1_sc, p1_sc, a1_sc))
    n_chunks = (t - ROW_BLOCK) // KV_CHUNK

    def result(s):
        l = acc_sc[s, LANES:LANES + SUBLANES, :]
        o_t = acc_sc[s, 0:LANES, :].reshape(LANES // SUBLANES, SUBLANES, tq) / l[None]
        return o_t.reshape(LANES, tq).T

    def finish():
        o0 = result(0)
        o1 = result(1)
        if diff:
            lv = lamv_ref[...]
            lam = (jnp.exp(jnp.sum(lv[0:1] * lv[1:2], axis=-1, keepdims=True))
                   - jnp.exp(jnp.sum(lv[2:3] * lv[3:4], axis=-1, keepdims=True)) + lam_init)
            o = o0 - lam * o1
            o_ref[...] = _rms(o, subln_ref[...]) * (1.0 - lam_init)
        else:
            lane = lax.broadcasted_iota(jnp.int32, (1, LANES), 1)
            o_ref[...] = jnp.where(lane < MLA_V, o0, o1)

    m_sc[...] = jnp.full(m_sc.shape, NEG_BIG, F32)
    acc_sc[...] = jnp.zeros(acc_sc.shape, F32)

    scores(s1_sc, c1_sc, ROW_BLOCK, KV_CHUNK)

    @pl.when(i == 0)
    def _():
        scores(s0_sc, c0_sc, 0, ROW_BLOCK)
        softmax(s0_sc, c0_sc, p0_sc, a0_sc, ROW_BLOCK)
        accumulate(p0_sc, a0_sc, 0, ROW_BLOCK)
        finish()

    @pl.when(i > 0)
    def _():
        for n in range(1, n_chunks + 1):
            s_a, c_a, p_a, a_a = bufs[(n - 1) % 2]
            s_b, c_b, p_b, a_b = bufs[n % 2]
            if n < n_chunks:
                scores(s_a, c_a, ROW_BLOCK + n * KV_CHUNK, KV_CHUNK)
            else:
                scores(s_a, c_a, 0, ROW_BLOCK)
            if n > 1:
                accumulate(p_a, a_a, ROW_BLOCK + (n - 2) * KV_CHUNK, KV_CHUNK)
            softmax(s_b, c_b, p_b, a_b, KV_CHUNK)
        s_a, c_a, p_a, a_a = bufs[n_chunks % 2]
        s_b, c_b, p_b, a_b = bufs[(n_chunks + 1) % 2]
        accumulate(p_a, a_a, t - KV_CHUNK, KV_CHUNK)
        softmax(s_b, c_b, p_b, a_b, ROW_BLOCK)
        accumulate(p_b, a_b, 0, ROW_BLOCK)
        finish()


def _flash(q, k, vt, extra, *, diff, lam_init=0.0):
    groups, t = vt.shape[0], vt.shape[2]
    nb = t // ROW_BLOCK
    assert (t - ROW_BLOCK) % (2 * KV_CHUNK) == 0
    if diff:
        q_spec = pl.BlockSpec((None, 2, ROW_BLOCK, LANES), lambda g, i: (g, 0, i, 0))
        k_spec = pl.BlockSpec((None, t, LANES), lambda g, i: (g, 0, 0))
        extra_specs = [pl.BlockSpec(e.shape, lambda g, i: (0, 0)) for e in extra]
    else:
        q_spec = pl.BlockSpec((2, ROW_BLOCK, LANES), lambda g, i: (g, i, 0))
        k_spec = pl.BlockSpec((2, t, LANES), lambda g, i: (g, 0, 0))
        extra_specs = []
    return pl.pallas_call(
        functools.partial(_flash_kernel, diff=diff, lam_init=lam_init),
        grid=(groups, nb),
        in_specs=[q_spec, k_spec, pl.BlockSpec((None, VT_ROWS, t), lambda g, i: (g, 0, 0))] + extra_specs,
        out_specs=pl.BlockSpec((ROW_BLOCK, LANES), lambda g, i: (i, g)),
        out_shape=jax.ShapeDtypeStruct((t, groups * LANES), F32),
        scratch_shapes=[pltpu.VMEM((2, SUBLANES, ROW_BLOCK), F32), pltpu.VMEM((2, VT_ROWS, ROW_BLOCK), F32)]
        + [pltpu.VMEM((2, KV_CHUNK, ROW_BLOCK), F32)] * 2 + [pltpu.VMEM((2, KV_CHUNK, ROW_BLOCK), BF16)] * 2
        + [pltpu.VMEM((2, SUBLANES, ROW_BLOCK), F32)] * 4,
        compiler_params=_params(("parallel", "arbitrary")),
        name="flash_diff" if diff else "flash_mla",
    )(q, k, vt, *extra)


NA_ROWS_PER_STEP = ROW_BLOCK // GRID_W


def _na_kernel(q_ref, k_ref, v_ref, bt_ref, o_ref, *, grid_rows):
    i = pl.program_id(1)
    lane = lax.broadcasted_iota(jnp.int32, (1, LANES), 1)
    first = lane < NA_HD
    kctx = k_ref[0:ROW_BLOCK, :]
    vctx = v_ref[0:ROW_BLOCK, :]

    def stacked(q):
        zero = jnp.zeros_like(q)
        return jnp.concatenate([jnp.where(first, q, zero), jnp.where(first, zero, q)], axis=0)

    @pl.when(i == 0)
    def _():
        q2 = stacked(q_ref[...])
        sc = _dot_nt(q2, kctx)
        p = jnp.exp(sc - jnp.max(sc, axis=-1, keepdims=True))
        o = _dot(p.astype(BF16), vctx) / jnp.sum(p, axis=-1, keepdims=True)
        o_ref[...] = jnp.where(first, o[0:ROW_BLOCK], o[ROW_BLOCK:])

    @pl.when(i > 0)
    def _():
        nkeys = WIN_R * GRID_W
        starts, s_lat, s_ctx = [], [], []
        for rr in range(NA_ROWS_PER_STEP):
            r = (i - 1) * NA_ROWS_PER_STEP + rr
            r0 = jnp.clip(r - WIN_R // 2, 0, grid_rows - WIN_R)
            start = pl.multiple_of(ROW_BLOCK + r0 * GRID_W, GRID_W)
            q2 = stacked(q_ref[rr * GRID_W:(rr + 1) * GRID_W, :])
            starts.append(start)
            s_lat.append(_dot_nt(q2, k_ref[pl.ds(start, nkeys), :]) + bt_ref[r0 - r + WIN_R - 1])
            s_ctx.append(_dot_nt(q2, kctx))
        p_lat, p_ctx, denom = [], [], []
        for rr in range(NA_ROWS_PER_STEP):
            m = jnp.maximum(jnp.max(s_lat[rr], axis=-1, keepdims=True), jnp.max(s_ctx[rr], axis=-1, keepdims=True))
            pl_, pc_ = jnp.exp(s_lat[rr] - m), jnp.exp(s_ctx[rr] - m)
            denom.append(jnp.sum(pl_, axis=-1, keepdims=True) + jnp.sum(pc_, axis=-1, keepdims=True))
            p_lat.append(pl_.astype(BF16))
            p_ctx.append(pc_.astype(BF16))
        for rr in range(NA_ROWS_PER_STEP):
            o = (_dot(p_lat[rr], v_ref[pl.ds(starts[rr], nkeys), :]) + _dot(p_ctx[rr], vctx)) / denom[rr]
            o_ref[rr * GRID_W:(rr + 1) * GRID_W, :] = jnp.where(first, o[0:GRID_W], o[GRID_W:])


def _na_attention(q, k, v, bias):
    npair, t = q.shape[0], q.shape[1]
    nb = t // ROW_BLOCK
    grid_rows = (t - ROW_BLOCK) // GRID_W
    resident = pl.BlockSpec((None, t, LANES), lambda p, i: (p, 0, 0))
    return pl.pallas_call(
        functools.partial(_na_kernel, grid_rows=grid_rows),
        grid=(npair, nb),
        in_specs=[pl.BlockSpec((None, ROW_BLOCK, LANES), lambda p, i: (p, i, 0)), resident, resident,
                  pl.BlockSpec((None,) + bias.shape[1:], lambda p, i: (p, 0, 0, 0))],
        out_specs=pl.BlockSpec((ROW_BLOCK, LANES), lambda p, i: (i, p)),
        out_shape=jax.ShapeDtypeStruct((t, npair * LANES), F32),
        compiler_params=_params(("parallel", "arbitrary")),
        name="na_attention",
    )(q, k, v, bias)


def _na_bias_table(rpb):
    w = np.arange(GRID_W)
    c0 = np.clip(w - WIN_C // 2, 0, GRID_W - WIN_C)
    inside = (w[None, :] >= c0[:, None]) & (w[None, :] < c0[:, None] + WIN_C)
    edge = GRID_W - WIN_C
    pad = jnp.pad(rpb, ((0, 0), (0, 0), (edge, edge)))
    col = jnp.stack([pad[:, :, GRID_W - 1 - wq:2 * GRID_W - 1 - wq] for wq in range(GRID_W)], axis=2)
    col = jnp.where(inside[None, None], col, NEG_BIG)
    tab = jnp.stack([col[:, var:var + WIN_R] for var in range(WIN_R)], axis=1)
    tab = tab.transpose(0, 1, 3, 2, 4).reshape(NA_HEADS // 2, 2, WIN_R, GRID_W, WIN_R * GRID_W)
    return tab.transpose(0, 2, 1, 3, 4).reshape(NA_HEADS // 2, WIN_R, 2 * GRID_W, WIN_R * GRID_W)


def _out_kernel(*refs, final):
    if final:
        a_ref, b_ref, gate_ref, w_ref, x_ref, mod_ref, fg_ref, o_ref = refs
    else:
        a_ref, b_ref, gate_ref, w_ref, x_ref, mod_ref, o_ref = refs
    half = a_ref.shape[-1]
    d = x_ref.shape[-1]
    ga = gate_ref[:, 0:half]
    gb = gate_ref[:, half:2 * half]
    ya = (a_ref[...] * (ga * jax.nn.sigmoid(ga))).astype(BF16)
    yb = (b_ref[...] * (gb * jax.nn.sigmoid(gb))).astype(BF16)
    y = _dot(ya, w_ref[0:half, :]) + _dot(yb, w_ref[half:2 * half, :])
    x_new = x_ref[...] + mod_ref[:, 2 * d:3 * d] * y
    o_ref[...] = _rms(x_new, fg_ref[...]) if final else x_new


def _out_proj(a, b, gate, w, x, mod, final_g=None):
    t, d = x.shape
    nb = t // ROW_BLOCK
    final = final_g is not None
    rows = lambda c: pl.BlockSpec((ROW_BLOCK, c), lambda i: (i, 0))
    in_specs = [rows(a.shape[1]), rows(b.shape[1]), rows(gate.shape[1]),
                pl.BlockSpec(w.shape, lambda i: (0, 0)), rows(d),
                pl.BlockSpec((None, 1, 3 * d), lambda i: (jnp.minimum(i, 1), 0, 0))]
    args = [a, b, gate, w, x, mod]
    if final:
        in_specs.append(pl.BlockSpec((1, d), lambda i: (0, 0)))
        args.append(final_g)
    return pl.pallas_call(
        functools.partial(_out_kernel, final=final),
        grid=(nb,),
        in_specs=in_specs,
        out_specs=pl.BlockSpec((ROW_BLOCK, d), lambda i: (jnp.maximum(i - 1, 0), 0)) if final else rows(d),
        out_shape=jax.ShapeDtypeStruct((t - ROW_BLOCK if final else t, d), F32),
        compiler_params=_params(("arbitrary",) if final else ("parallel",)),
        name="out_proj_final" if final else "out_proj",
    )(*args)


def _lru_kernel(*refs, reverse, nb):
    if reverse:
        (x_ref, xp_ref, xn_ref, cw_ref, cb_ref, lam_ref, wa_ref, ba_ref, wx_ref, bx_ref, hf_ref,
         o_ref, xs_sc, a_sc, b_sc, carry_sc) = refs
    else:
        (x_ref, xp_ref, xn_ref, cw_ref, cb_ref, lam_ref, wa_ref, ba_ref, wx_ref, bx_ref,
         o_ref, xs_sc, a_sc, b_sc, carry_sc) = refs
    step = pl.program_id(0)
    blk = jnp.where(step == 0, 0, nb - step) if reverse else step
    tb = x_ref.shape[0]

    @pl.when(step == 0)
    def _():
        carry_sc[...] = jnp.zeros_like(carry_sc)

    prev_ok = blk >= 2
    next_ok = (blk >= 1) & (blk < nb - 1)
    xs_sc[0:SUBLANES, :] = jnp.where(prev_ok, xp_ref[...], 0.0)
    xs_sc[SUBLANES:SUBLANES + tb, :] = x_ref[...]
    xs_sc[SUBLANES + tb:2 * SUBLANES + tb, :] = jnp.where(next_ok, xn_ref[...], 0.0)
    u = cb_ref[...]
    for j in range(CONV_W):
        off = SUBLANES + j - CONV_W // 2
        u = u + xs_sc[off:off + tb, :] * cw_ref[j:j + 1, :]

    ub = u.astype(BF16)
    r = jax.nn.sigmoid(_dot(ub, wa_ref[...]) + ba_ref[...])
    gi = jax.nn.sigmoid(_dot(ub, wx_ref[...]) + bx_ref[...])
    nl = -lam_ref[...]
    softplus = jnp.maximum(nl, 0.0) + jnp.log1p(jnp.exp(-jnp.abs(nl)))
    log_a = -LRU_C * r * softplus
    a_sc[...] = jnp.exp(log_a)
    b_sc[...] = jnp.sqrt(1.0 - jnp.exp(2.0 * log_a)) * (gi * u)

    row = lax.broadcasted_iota(jnp.int32, (SUBLANES, 1), 0)
    nchunk = tb // SUBLANES

    def chunk(c, carry):
        cc = nchunk - 1 - c if reverse else c
        start = pl.multiple_of(cc * SUBLANES, SUBLANES)
        a = a_sc[pl.ds(start, SUBLANES), :]
        b = b_sc[pl.ds(start, SUBLANES), :]
        for k in (1, 2, 4):
            shift = SUBLANES - k if reverse else k
            valid = (row < SUBLANES - k) if reverse else (row >= k)
            a_sh = pltpu.roll(a, shift, 0)
            b_sh = pltpu.roll(b, shift, 0)
            b = jnp.where(valid, a * b_sh + b, b)
            a = jnp.where(valid, a * a_sh, a)
        h = b + a * carry
        if reverse:
            o_ref[pl.ds(start, SUBLANES), :] = h + hf_ref[pl.ds(start, SUBLANES), :]
            new = h[0:1, :]
        else:
            o_ref[pl.ds(start, SUBLANES), :] = h
            new = h[SUBLANES - 1:SUBLANES, :]
        return jnp.broadcast_to(new, carry.shape)

    carry_sc[...] = lax.fori_loop(0, nchunk, chunk, carry_sc[...])


def _lru_pass(x, conv_w, conv_b, lam, wa, ba, wx, bx, hf, *, reverse):
    t, c = x.shape
    tb = ROW_BLOCK
    nb = t // tb
    per8 = tb // SUBLANES
    nb8 = t // SUBLANES
    if reverse:
        blk = lambda s: jnp.where(s == 0, 0, nb - s)
    else:
        blk = lambda s: s
    full = lambda a: pl.BlockSpec(a.shape, lambda s: (0,) * a.ndim)
    main = pl.BlockSpec((tb, c), lambda s: (blk(s), 0))
    args = [x, x, x, conv_w, conv_b, lam, wa, ba, wx, bx]
    in_specs = [main,
                pl.BlockSpec((SUBLANES, c), lambda s: (jnp.maximum(blk(s) * per8 - 1, 0), 0)),
                pl.BlockSpec((SUBLANES, c), lambda s: (jnp.minimum((blk(s) + 1) * per8, nb8 - 1), 0)),
                full(conv_w), full(conv_b), full(lam), full(wa), full(ba), full(wx), full(bx)]
    if reverse:
        args.append(hf)
        in_specs.append(main)
    return pl.pallas_call(
        functools.partial(_lru_kernel, reverse=reverse, nb=nb),
        grid=(nb,),
        in_specs=in_specs,
        out_specs=main,
        out_shape=jax.ShapeDtypeStruct((t, c), F32),
        scratch_shapes=[pltpu.VMEM((tb + 2 * SUBLANES, c), F32), pltpu.VMEM((tb, c), F32),
                        pltpu.VMEM((tb, c), F32), pltpu.VMEM((SUBLANES, c), F32)],
        compiler_params=_params(("arbitrary",)),
        name="lru_bwd" if reverse else "lru_fwd",
    )(*args)


def _block_diag(w):
    g, bw, _ = w.shape
    eye = jnp.eye(g, dtype=w.dtype)
    return (eye[:, None, :, None] * w[:, :, None, :]).reshape(g * bw, g * bw)


def _rope_tables(n, ctx_len, rot_dim, lead, tail, repeat):
    t = jnp.arange(n)
    row = (t // GRID_W).astype(F32)
    col = (t % GRID_W).astype(F32)
    n_freq = rot_dim // 4
    inv = ROPE_BASE ** (-jnp.arange(n_freq, dtype=F32) / n_freq)
    ang = jnp.concatenate([row[:, None] * inv, col[:, None] * inv], axis=-1)
    cos, sin = jnp.cos(ang), jnp.sin(ang)
    cos_l = jnp.concatenate([jnp.ones((n, lead), F32)] + [cos, cos] * repeat + [jnp.ones((n, tail), F32)], axis=-1)
    sin_l = jnp.concatenate([jnp.zeros((n, lead), F32)] + [-sin, sin] * repeat + [jnp.zeros((n, tail), F32)], axis=-1)
    cos_t = jnp.concatenate([jnp.ones((ctx_len, LANES), F32), cos_l], axis=0)
    sin_t = jnp.concatenate([jnp.zeros((ctx_len, LANES), F32), sin_l], axis=0)
    return cos_t, sin_t


def _even_weights(w_in, w_uq, w_ukv):
    d = w_in.shape[0]
    q, k, v, g_na, lq, lkv, kr, g_mla = jnp.split(
        w_in, [512, 1024, 1536, 2048, 2048 + Q_LORA, 2048 + Q_LORA + KV_LORA, 2048 + Q_LORA + KV_LORA + MLA_ROPE], axis=1)
    half = MLA_ROPE // 2
    pad = LANES - MLA_QK
    kr_pad = jnp.concatenate([jnp.zeros((d, MLA_NOPE), F32), kr, jnp.zeros((d, pad), F32)], axis=1)
    krs_pad = jnp.concatenate([jnp.zeros((d, MLA_NOPE), F32), kr[:, half:], kr[:, :half], jnp.zeros((d, pad), F32)], axis=1)
    w = jnp.concatenate([q * (NA_HD ** -0.5), k, v, g_na, g_mla, lq, lkv, kr_pad, krs_pad], axis=1).astype(BF16)
    uq = w_uq.reshape(Q_LORA, MLA_HEADS, MLA_QK)
    zq = jnp.zeros((Q_LORA, MLA_HEADS, pad), F32)
    uq_main = jnp.concatenate([uq, zq], axis=-1)
    uq_swap = jnp.concatenate([jnp.zeros((Q_LORA, MLA_HEADS, MLA_NOPE), F32), uq[..., MLA_NOPE + half:],
                               uq[..., MLA_NOPE:MLA_NOPE + half], zq], axis=-1)
    wuq = jnp.concatenate([uq_main.reshape(Q_LORA, -1), uq_swap.reshape(Q_LORA, -1)], axis=1).astype(BF16)
    ukv = w_ukv.reshape(KV_LORA, MLA_HEADS, MLA_NOPE + MLA_V)
    k_nope = jnp.concatenate([ukv[..., :MLA_NOPE], jnp.zeros((KV_LORA, MLA_HEADS, LANES - MLA_NOPE), F32)], axis=-1)
    wkv = jnp.concatenate([k_nope.reshape(KV_LORA, -1), ukv[..., MLA_NOPE:].reshape(KV_LORA, -1)], axis=1).astype(BF16)
    return w, wuq, wkv


def _odd_weights(w_in):
    d = w_in.shape[0]
    q, k, v, g_d, u, g_lru = jnp.split(w_in, [512, 1024, 1536, 2048, 2560], axis=1)

    def swap(m):
        return m.reshape(d, DIFF_HEADS, 2, 2, DIFF_HD // 2)[:, :, :, ::-1, :].reshape(d, DIFF_W)

    scale = DIFF_HD ** -0.5
    return jnp.concatenate([q * scale, swap(q) * scale, k, swap(k), v, g_d, g_lru, u], axis=1).astype(BF16)


def kernel(x, c, ctx, c_ctx, mod_w, mod_b, norm_g, final_g, e_w_in, e_w_out, na_rpb, mla_q_norm, mla_w_uq, mla_kv_norm, mla_w_ukv, o_w_in, o_w_out, diff_lq1, diff_lk1, diff_lq2, diff_lk2, diff_subln, lru_conv_w, lru_conv_b, lru_lambda, lru_wa, lru_ba, lru_wx, lru_bx):
    batch, n, d = x.shape
    ctx_len = ctx.shape[1]
    assert batch == 1 and ctx_len == ROW_BLOCK and d == D_MODEL
    assert n // GRID_W >= WIN_R

    cvec = jnp.concatenate([c_ctx[None], c, jnp.zeros((SUBLANES - 2, d), F32)], axis=0)
    mods = _modulation(cvec, mod_w, mod_b)[:, 0:2].reshape(DEPTH, 2, 1, 3 * d)

    cos_m, sin_m = _rope_tables(n, ctx_len, MLA_ROPE, MLA_NOPE, LANES - MLA_QK, 1)
    cos_d, sin_d = _rope_tables(n, ctx_len, DIFF_HD, 0, 0, 2)

    xs = jnp.concatenate([ctx[0], x[0]], axis=0)
    for l in range(DEPTH):
        i = l // 2
        g_row = norm_g[l][None]
        if l % 2 == 0:
            w, wuq, wkv = _even_weights(e_w_in[i], mla_w_uq[i], mla_w_ukv[i])
            naq, nak, nav, gate, mq, mk, mv = _in_even(
                xs, g_row, mods[l], w, mla_q_norm[i][None], wuq, mla_kv_norm[i][None], wkv, cos_m, sin_m)
            a = _na_attention(naq, nak, nav, _na_bias_table(na_rpb[i]))
            b = _flash(mq, mk, mv, (), diff=False)
            w_out = e_w_out[i]
        else:
            lam_init = 0.8 - 0.6 * math.exp(-0.3 * l)
            dq, dk, dv, gate, u = _in_odd(xs, g_row, mods[l], _odd_weights(o_w_in[i]), cos_d, sin_d)
            lamv = jnp.stack([diff_lq1[i], diff_lk1[i], diff_lq2[i], diff_lk2[i]])
            a = _flash(dq, dk, dv, (lamv, diff_subln[i][None]), diff=True, lam_init=lam_init)
            lru = lambda dr, hf: _lru_pass(
                u, lru_conv_w[i], lru_conv_b[i][None], lru_lambda[i, dr][None],
                _block_diag(lru_wa[i, dr]).astype(BF16), lru_ba[i, dr][None],
                _block_diag(lru_wx[i, dr]).astype(BF16), lru_bx[i, dr][None], hf, reverse=dr == 1)
            b = lru(1, lru(0, None))
            w_out = o_w_out[i]
        xs = _out_proj(a, b, gate, w_out.astype(BF16), xs, mods[l], final_g[None] if l == DEPTH - 1 else None)
    return xs[None]
```

```python
import functools
import math

import jax
import jax.numpy as jnp
import numpy as np
from jax import lax
from jax.experimental import pallas as pl
from jax.experimental.pallas import tpu as pltpu

F32 = jnp.float32
BF16 = jnp.bfloat16

D_MODEL = 1024
DEPTH = 4
GRID_W = 64
EPS = 1e-6
ROPE_BASE = 10000.0

NA_HEADS = 8
NA_HD = 64
NA_W = NA_HEADS * NA_HD
WIN_R = 8
WIN_C = 16

MLA_HEADS = 8
MLA_NOPE = 64
MLA_ROPE = 32
MLA_V = 64
MLA_QK = MLA_NOPE + MLA_ROPE
MLA_W = MLA_HEADS * MLA_V
Q_LORA = 384
KV_LORA = 256

DIFF_HEADS = 4
DIFF_HD = 64
DIFF_W = DIFF_HEADS * 2 * DIFF_HD

LRU_W = 512
LRU_BLOCKS = 8
LRU_BW = LRU_W // LRU_BLOCKS
CONV_W = 4
LRU_C = 8.0

LANES = 128
SUBLANES = 8
ROW_BLOCK = 256
KV_CHUNK = 1024
SOFTMAX_ROWS = 256
VT_ROWS = LANES + 16
VMEM_LIMIT = 56 * 1024 * 1024
NEG_BIG = -1e30
LOG2E = math.log2(math.e)

_NT = (((1,), (1,)), ((), ()))


def _dot_nt(a, b):
    return lax.dot_general(a, b, _NT, preferred_element_type=F32)


def _dot(a, b):
    return jnp.dot(a, b, preferred_element_type=F32)


def _params(sem):
    return pltpu.CompilerParams(dimension_semantics=sem, vmem_limit_bytes=VMEM_LIMIT)


def _mod_kernel(c_ref, w_ref, b_ref, o_ref):
    c = c_ref[...]
    s = c * jax.nn.sigmoid(c)
    o_ref[...] = _dot(s.astype(BF16), w_ref[...].astype(BF16)) + b_ref[...]


def _modulation(cvec, mod_w, mod_b):
    depth, d, d3 = mod_w.shape
    tn = 512
    return pl.pallas_call(
        _mod_kernel,
        grid=(depth, d3 // tn),
        in_specs=[
            pl.BlockSpec((SUBLANES, d), lambda l, j: (0, 0)),
            pl.BlockSpec((None, d, tn), lambda l, j: (l, 0, j)),
            pl.BlockSpec((None, 1, tn), lambda l, j: (l, 0, j)),
        ],
        out_specs=pl.BlockSpec((None, SUBLANES, tn), lambda l, j: (l, 0, j)),
        out_shape=jax.ShapeDtypeStruct((depth, SUBLANES, d3), F32),
        compiler_params=_params(("parallel", "parallel")),
        name="modulation",
    )(cvec, mod_w, mod_b.reshape(depth, 1, d3))


def _norm_mod(x, g, mod):
    d = x.shape[-1]
    y = x * lax.rsqrt(jnp.mean(x * x, axis=-1, keepdims=True) + EPS) * g
    return y * (1.0 + mod[:, d:2 * d]) + mod[:, 0:d]


def _rms(x, g):
    return x * lax.rsqrt(jnp.mean(x * x, axis=-1, keepdims=True) + EPS) * g


E_Q, E_K, E_V, E_G, E_LQ, E_LKV, E_KR, E_KRS, E_END = 0, 512, 1024, 1536, 2560, 2944, 3200, 3328, 3456


def _in_even_kernel(x_ref, g_ref, mod_ref, w_ref, qn_ref, wuq_ref, kvn_ref, wkv_ref, c_ref, s_ref,
                    naq_ref, nak_ref, nav_ref, gate_ref, mq_ref, mk_ref, mv_ref):
    hb = _norm_mod(x_ref[...], g_ref[...], mod_ref[...]).astype(BF16)
    npair = NA_HEADS // 2
    for ref, off in ((naq_ref, E_Q), (nak_ref, E_K), (nav_ref, E_V)):
        seg = _dot(hb, w_ref[:, off:off + NA_W])
        for p in range(npair):
            ref[p] = seg[:, p * LANES:(p + 1) * LANES].astype(BF16)
    gate_ref[...] = _dot(hb, w_ref[:, E_G:E_LQ])
    cos = c_ref[...]
    sin = s_ref[...]
    qn = _rms(_dot(hb, w_ref[:, E_LQ:E_LKV]), qn_ref[...]).astype(BF16)
    width = MLA_HEADS * LANES
    q_main = _dot(qn, wuq_ref[:, 0:width])
    q_swap = _dot(qn, wuq_ref[:, width:2 * width])
    scale = MLA_QK ** -0.5 * LOG2E
    for h in range(MLA_HEADS):
        sl = slice(h * LANES, (h + 1) * LANES)
        mq_ref[h] = ((q_main[:, sl] * cos + q_swap[:, sl] * sin) * scale).astype(BF16)
    kvn = _rms(_dot(hb, w_ref[:, E_LKV:E_KR]), kvn_ref[...]).astype(BF16)
    kr = _dot(hb, w_ref[:, E_KR:E_KRS]) * cos + _dot(hb, w_ref[:, E_KRS:E_END]) * sin
    k_nope = _dot(kvn, wkv_ref[:, 0:width])
    for h in range(MLA_HEADS):
        mk_ref[h] = (k_nope[:, h * LANES:(h + 1) * LANES] + kr).astype(BF16)
    v = _dot(kvn, wkv_ref[:, width:width + MLA_W])
    for p in range(MLA_HEADS // 2):
        mv_ref[p, 0:LANES, :] = v[:, p * LANES:(p + 1) * LANES].T.astype(BF16)
    mv_ref[:, LANES:, :] = jnp.ones((MLA_HEADS // 2, VT_ROWS - LANES, ROW_BLOCK), BF16)


def _in_even(x, norm_g, mod, w, qn, wuq, kvn, wkv, cos, sin):
    t, d = x.shape
    nb = t // ROW_BLOCK
    full = lambda a: pl.BlockSpec(a.shape, lambda i: (0,) * a.ndim)
    rows = lambda c: pl.BlockSpec((ROW_BLOCK, c), lambda i: (i, 0))
    heads = lambda n: pl.BlockSpec((n, ROW_BLOCK, LANES), lambda i: (0, i, 0))
    hshape = lambda n: jax.ShapeDtypeStruct((n, t, LANES), BF16)
    return pl.pallas_call(
        _in_even_kernel,
        grid=(nb,),
        in_specs=[rows(d), full(norm_g),
                  pl.BlockSpec((None, 1, 3 * d), lambda i: (jnp.minimum(i, 1), 0, 0)),
                  full(w), full(qn), full(wuq), full(kvn), full(wkv), rows(LANES), rows(LANES)],
        out_specs=[heads(4), heads(4), heads(4), rows(2 * NA_W), heads(8), heads(8),
                   pl.BlockSpec((4, VT_ROWS, ROW_BLOCK), lambda i: (0, 0, i))],
        out_shape=[hshape(4), hshape(4), hshape(4), jax.ShapeDtypeStruct((t, 2 * NA_W), F32),
                   hshape(8), hshape(8), jax.ShapeDtypeStruct((4, VT_ROWS, t), BF16)],
        compiler_params=_params(("parallel",)),
        name="in_proj_even",
    )(x, norm_g, mod, w, qn, wuq, kvn, wkv, cos, sin)


O_Q, O_QS, O_K, O_KS, O_V, O_G, O_U, O_END = 0, 512, 1024, 1536, 2048, 2560, 3584, 4096


def _in_odd_kernel(x_ref, g_ref, mod_ref, w_ref, c_ref, s_ref, dq_ref, dk_ref, dv_ref, gate_ref, u_ref):
    hb = _norm_mod(x_ref[...], g_ref[...], mod_ref[...]).astype(BF16)
    cos = c_ref[...]
    sin = s_ref[...]
    lane = lax.broadcasted_iota(jnp.int32, (1, LANES), 1)
    first = lane < DIFF_HD
    q = _dot(hb, w_ref[:, O_Q:O_QS])
    qs = _dot(hb, w_ref[:, O_QS:O_K])
    k = _dot(hb, w_ref[:, O_K:O_KS])
    ks = _dot(hb, w_ref[:, O_KS:O_V])
    v = _dot(hb, w_ref[:, O_V:O_G])
    for h in range(DIFF_HEADS):
        sl = slice(h * LANES, (h + 1) * LANES)
        qr = (q[:, sl] * cos + qs[:, sl] * sin) * LOG2E
        dq_ref[h, 0] = jnp.where(first, qr, 0.0).astype(BF16)
        dq_ref[h, 1] = jnp.where(first, 0.0, qr).astype(BF16)
        dk_ref[h] = (k[:, sl] * cos + ks[:, sl] * sin).astype(BF16)
        dv_ref[h, 0:LANES, :] = v[:, sl].T.astype(BF16)
    dv_ref[:, LANES:, :] = jnp.ones((DIFF_HEADS, VT_ROWS - LANES, ROW_BLOCK), BF16)
    gate_ref[...] = _dot(hb, w_ref[:, O_G:O_U])
    u_ref[...] = _dot(hb, w_ref[:, O_U:O_END])


def _in_odd(x, norm_g, mod, w, cos, sin):
    t, d = x.shape
    nb = t // ROW_BLOCK
    full = lambda a: pl.BlockSpec(a.shape, lambda i: (0,) * a.ndim)
    rows = lambda c: pl.BlockSpec((ROW_BLOCK, c), lambda i: (i, 0))
    heads = pl.BlockSpec((DIFF_HEADS, ROW_BLOCK, LANES), lambda i: (0, i, 0))
    hshape = jax.ShapeDtypeStruct((DIFF_HEADS, t, LANES), BF16)
    return pl.pallas_call(
        _in_odd_kernel,
        grid=(nb,),
        in_specs=[rows(d), full(norm_g),
                  pl.BlockSpec((None, 1, 3 * d), lambda i: (jnp.minimum(i, 1), 0, 0)),
                  full(w), rows(LANES), rows(LANES)],
        out_specs=[pl.BlockSpec((DIFF_HEADS, 2, ROW_BLOCK, LANES), lambda i: (0, 0, i, 0)),
                   heads, pl.BlockSpec((DIFF_HEADS, VT_ROWS, ROW_BLOCK), lambda i: (0, 0, i)),
                   rows(DIFF_W + LRU_W), rows(LRU_W)],
        out_shape=[jax.ShapeDtypeStruct((DIFF_HEADS, 2, t, LANES), BF16), hshape,
                   jax.ShapeDtypeStruct((DIFF_HEADS, VT_ROWS, t), BF16),
                   jax.ShapeDtypeStruct((t, DIFF_W + LRU_W), F32), jax.ShapeDtypeStruct((t, LRU_W), F32)],
        compiler_params=_params(("parallel",)),
        name="in_proj_odd",
    )(x, norm_g, mod, w, cos, sin)


def _sublane_all(op, x):
    for k in (4, 2, 1):
        x = op(x, pltpu.roll(x, k, 0))
    return x


def _flash_kernel(*refs, diff, lam_init):
    if diff:
        q_ref, qn_ref, k_ref, vt_ref, lamv_ref, subln_ref, o_ref = refs[:7]
    else:
        q_ref, qn_ref, k_ref, vt_ref, o_ref = refs[:5]
    m_sc, acc_sc, s0_sc, s1_sc, p0_sc, p1_sc, a0_sc, a1_sc, c0_sc, c1_sc = refs[-10:]
    i = pl.program_id(1)
    tq = q_ref.shape[1]
    dv = vt_ref.shape[0]
    t = vt_ref.shape[1]

    def scores(s_sc, c_sc, start, size, queries=q_ref):
        for s in range(2):
            kc = k_ref[pl.ds(start, size), :] if diff else k_ref[s, pl.ds(start, size), :]
            st = _dot_nt(kc, queries[s])
            s_sc[s, 0:size, :] = st
            c_sc[s] = jnp.max(st.reshape(size // SUBLANES, SUBLANES, tq), axis=0)

    def softmax(s_sc, c_sc, p_sc, a_sc, size):
        for s in range(2):
            m_prev = m_sc[s]
            m_new = jnp.maximum(m_prev, _sublane_all(jnp.maximum, c_sc[s]))
            alpha = jnp.exp2(m_prev - m_new)
            for r in range(0, size, SOFTMAX_ROWS):
                x = s_sc[s, r:r + SOFTMAX_ROWS, :].reshape(SOFTMAX_ROWS // SUBLANES, SUBLANES, tq) - m_new[None]
                p = jnp.exp2(x)
                p_sc[s, r:r + SOFTMAX_ROWS, :] = p.reshape(SOFTMAX_ROWS, tq).astype(BF16)
            a_sc[s] = alpha
            m_sc[s] = m_new

    def accumulate(p_sc, a_sc, start, size):
        vt = vt_ref[:, pl.ds(start, size)]
        for s in range(2):
            acc = acc_sc[s].reshape(dv // SUBLANES, SUBLANES, tq) * a_sc[s][None]
            acc_sc[s] = acc.reshape(dv, tq) + _dot(vt, p_sc[s, 0:size, :])

    bufs = ((s0_sc, c0_sc, p0_sc, a0_sc), (s1_sc, c1_sc, p1_sc, a1_sc))
    n_chunks = (t - ROW_BLOCK) // KV_CHUNK

    def result(s):
        l = acc_sc[s, LANES:LANES + SUBLANES, :]
        o_t = acc_sc[s, 0:LANES, :].reshape(LANES // SUBLANES, SUBLANES, tq) / l[None]
        return o_t.reshape(LANES, tq).T

    def finish():
        o0 = result(0)
        o1 = result(1)
        if diff:
            lv = lamv_ref[...]
            lam = (jnp.exp(jnp.sum(lv[0:1] * lv[1:2], axis=-1, keepdims=True))
                   - jnp.exp(jnp.sum(lv[2:3] * lv[3:4], axis=-1, keepdims=True)) + lam_init)
            o = o0 - lam * o1
            o_ref[...] = _rms(o, subln_ref[...]) * (1.0 - lam_init)
        else:
            lane = lax.broadcasted_iota(jnp.int32, (1, LANES), 1)
            o_ref[...] = jnp.where(lane < MLA_V, o0, o1)

    m_sc[...] = jnp.full(m_sc.shape, NEG_BIG, F32)
    acc_sc[...] = jnp.zeros(acc_sc.shape, F32)

    @pl.when(i == 0)
    def _():
        scores(s0_sc, c0_sc, 0, ROW_BLOCK)
        softmax(s0_sc, c0_sc, p0_sc, a0_sc, ROW_BLOCK)
        scores(s1_sc, c1_sc, ROW_BLOCK, KV_CHUNK, qn_ref)
        accumulate(p0_sc, a0_sc, 0, ROW_BLOCK)
        finish()

    @pl.when(i > 0)
    def _():
        for n in range(1, n_chunks + 1):
            s_a, c_a, p_a, a_a = bufs[(n - 1) % 2]
            s_b, c_b, p_b, a_b = bufs[n % 2]
            if n < n_chunks:
                scores(s_a, c_a, ROW_BLOCK + n * KV_CHUNK, KV_CHUNK)
            else:
                scores(s_a, c_a, 0, ROW_BLOCK)
            if n > 1:
                accumulate(p_a, a_a, ROW_BLOCK + (n - 2) * KV_CHUNK, KV_CHUNK)
            softmax(s_b, c_b, p_b, a_b, KV_CHUNK)
        s_a, c_a, p_a, a_a = bufs[n_chunks % 2]
        s_b, c_b, p_b, a_b = bufs[(n_chunks + 1) % 2]
        accumulate(p_a, a_a, t - KV_CHUNK, KV_CHUNK)
        softmax(s_b, c_b, p_b, a_b, ROW_BLOCK)
        scores(s1_sc, c1_sc, ROW_BLOCK, KV_CHUNK, qn_ref)
        accumulate(p_b, a_b, 0, ROW_BLOCK)
        finish()


def _flash(q, k, vt, extra, *, diff, lam_init=0.0):
    groups, t = vt.shape[0], vt.shape[2]
    nb = t // ROW_BLOCK
    assert (t - ROW_BLOCK) % (2 * KV_CHUNK) == 0
    if diff:
        q_spec = pl.BlockSpec((None, 2, ROW_BLOCK, LANES), lambda g, i: (g, 0, i, 0))
        qn_spec = pl.BlockSpec((None, 2, ROW_BLOCK, LANES), lambda g, i: (g, 0, jnp.minimum(i + 1, nb - 1), 0))
        k_spec = pl.BlockSpec((None, t, LANES), lambda g, i: (g, 0, 0))
        extra_specs = [pl.BlockSpec(e.shape, lambda g, i: (0, 0)) for e in extra]
    else:
        q_spec = pl.BlockSpec((2, ROW_BLOCK, LANES), lambda g, i: (g, i, 0))
        qn_spec = pl.BlockSpec((2, ROW_BLOCK, LANES), lambda g, i: (g, jnp.minimum(i + 1, nb - 1), 0))
        k_spec = pl.BlockSpec((2, t, LANES), lambda g, i: (g, 0, 0))
        extra_specs = []
    return pl.pallas_call(
        functools.partial(_flash_kernel, diff=diff, lam_init=lam_init),
        grid=(groups, nb),
        in_specs=[q_spec, qn_spec, k_spec, pl.BlockSpec((None, VT_ROWS, t), lambda g, i: (g, 0, 0))] + extra_specs,
        out_specs=pl.BlockSpec((ROW_BLOCK, LANES), lambda g, i: (i, g)),
        out_shape=jax.ShapeDtypeStruct((t, groups * LANES), F32),
        scratch_shapes=[pltpu.VMEM((2, SUBLANES, ROW_BLOCK), F32), pltpu.VMEM((2, VT_ROWS, ROW_BLOCK), F32)]
        + [pltpu.VMEM((2, KV_CHUNK, ROW_BLOCK), F32)] * 2 + [pltpu.VMEM((2, KV_CHUNK, ROW_BLOCK), BF16)] * 2
        + [pltpu.VMEM((2, SUBLANES, ROW_BLOCK), F32)] * 4,
        compiler_params=_params(("parallel", "arbitrary")),
        name="flash_diff" if diff else "flash_mla",
    )(q, q, k, vt, *extra)


NA_ROWS_PER_STEP = ROW_BLOCK // GRID_W


def _na_kernel(q_ref, k_ref, v_ref, bt_ref, o_ref, *, grid_rows):
    i = pl.program_id(1)
    lane = lax.broadcasted_iota(jnp.int32, (1, LANES), 1)
    first = lane < NA_HD
    kctx = k_ref[0:ROW_BLOCK, :]
    vctx = v_ref[0:ROW_BLOCK, :]

    def stacked(q):
        zero = jnp.zeros_like(q)
        return jnp.concatenate([jnp.where(first, q, zero), jnp.where(first, zero, q)], axis=0)

    @pl.when(i == 0)
    def _():
        q2 = stacked(q_ref[...])
        sc = _dot_nt(q2, kctx)
        p = jnp.exp(sc - jnp.max(sc, axis=-1, keepdims=True))
        o = _dot(p.astype(BF16), vctx) / jnp.sum(p, axis=-1, keepdims=True)
        o_ref[...] = jnp.where(first, o[0:ROW_BLOCK], o[ROW_BLOCK:])

    @pl.when(i > 0)
    def _():
        nkeys = WIN_R * GRID_W
        starts, s_lat, s_ctx = [], [], []
        for rr in range(NA_ROWS_PER_STEP):
            r = (i - 1) * NA_ROWS_PER_STEP + rr
            r0 = jnp.clip(r - WIN_R // 2, 0, grid_rows - WIN_R)
            start = pl.multiple_of(ROW_BLOCK + r0 * GRID_W, GRID_W)
            q2 = stacked(q_ref[rr * GRID_W:(rr + 1) * GRID_W, :])
            starts.append(start)
            s_lat.append(_dot_nt(q2, k_ref[pl.ds(start, nkeys), :]) + bt_ref[r0 - r + WIN_R - 1])
            s_ctx.append(_dot_nt(q2, kctx))
        p_lat, p_ctx, denom = [], [], []
        for rr in range(NA_ROWS_PER_STEP):
            m = jnp.maximum(jnp.max(s_lat[rr], axis=-1, keepdims=True), jnp.max(s_ctx[rr], axis=-1, keepdims=True))
            pl_, pc_ = jnp.exp(s_lat[rr] - m), jnp.exp(s_ctx[rr] - m)
            denom.append(jnp.sum(pl_, axis=-1, keepdims=True) + jnp.sum(pc_, axis=-1, keepdims=True))
            p_lat.append(pl_.astype(BF16))
            p_ctx.append(pc_.astype(BF16))
        for rr in range(NA_ROWS_PER_STEP):
            o = (_dot(p_lat[rr], v_ref[pl.ds(starts[rr], nkeys), :]) + _dot(p_ctx[rr], vctx)) / denom[rr]
            o_ref[rr * GRID_W:(rr + 1) * GRID_W, :] = jnp.where(first, o[0:GRID_W], o[GRID_W:])


def _na_attention(q, k, v, bias):
    npair, t = q.shape[0], q.shape[1]
    nb = t // ROW_BLOCK
    grid_rows = (t - ROW_BLOCK) // GRID_W
    resident = pl.BlockSpec((None, t, LANES), lambda p, i: (p, 0, 0))
    return pl.pallas_call(
        functools.partial(_na_kernel, grid_rows=grid_rows),
        grid=(npair, nb),
        in_specs=[pl.BlockSpec((None, ROW_BLOCK, LANES), lambda p, i: (p, i, 0)), resident, resident,
                  pl.BlockSpec((None,) + bias.shape[1:], lambda p, i: (p, 0, 0, 0))],
        out_specs=pl.BlockSpec((ROW_BLOCK, LANES), lambda p, i: (i, p)),
        out_shape=jax.ShapeDtypeStruct((t, npair * LANES), F32),
        compiler_params=_params(("parallel", "arbitrary")),
        name="na_attention",
    )(q, k, v, bias)


def _na_bias_table(rpb):
    w = np.arange(GRID_W)
    c0 = np.clip(w - WIN_C // 2, 0, GRID_W - WIN_C)
    inside = (w[None, :] >= c0[:, None]) & (w[None, :] < c0[:, None] + WIN_C)
    edge = GRID_W - WIN_C
    pad = jnp.pad(rpb, ((0, 0), (0, 0), (edge, edge)))
    col = jnp.stack([pad[:, :, GRID_W - 1 - wq:2 * GRID_W - 1 - wq] for wq in range(GRID_W)], axis=2)
    col = jnp.where(inside[None, None], col, NEG_BIG)
    tab = jnp.stack([col[:, var:var + WIN_R] for var in range(WIN_R)], axis=1)
    tab = tab.transpose(0, 1, 3, 2, 4).reshape(NA_HEADS // 2, 2, WIN_R, GRID_W, WIN_R * GRID_W)
    return tab.transpose(0, 2, 1, 3, 4).reshape(NA_HEADS // 2, WIN_R, 2 * GRID_W, WIN_R * GRID_W)


def _out_kernel(*refs, final):
    if final:
        a_ref, b_ref, gate_ref, w_ref, x_ref, mod_ref, fg_ref, o_ref = refs
    else:
        a_ref, b_ref, gate_ref, w_ref, x_ref, mod_ref, o_ref = refs
    half = a_ref.shape[-1]
    d = x_ref.shape[-1]
    ga = gate_ref[:, 0:half]
    gb = gate_ref[:, half:2 * half]
    ya = (a_ref[...] * (ga * jax.nn.sigmoid(ga))).astype(BF16)
    yb = (b_ref[...] * (gb * jax.nn.sigmoid(gb))).astype(BF16)
    y = _dot(ya, w_ref[0:half, :]) + _dot(yb, w_ref[half:2 * half, :])
    x_new = x_ref[...] + mod_ref[:, 2 * d:3 * d] * y
    o_ref[...] = _rms(x_new, fg_ref[...]) if final else x_new


def _out_proj(a, b, gate, w, x, mod, final_g=None):
    t, d = x.shape
    nb = t // ROW_BLOCK
    final = final_g is not None
    rows = lambda c: pl.BlockSpec((ROW_BLOCK, c), lambda i: (i, 0))
    in_specs = [rows(a.shape[1]), rows(b.shape[1]), rows(gate.shape[1]),
                pl.BlockSpec(w.shape, lambda i: (0, 0)), rows(d),
                pl.BlockSpec((None, 1, 3 * d), lambda i: (jnp.minimum(i, 1), 0, 0))]
    args = [a, b, gate, w, x, mod]
    if final:
        in_specs.append(pl.BlockSpec((1, d), lambda i: (0, 0)))
        args.append(final_g)
    return pl.pallas_call(
        functools.partial(_out_kernel, final=final),
        grid=(nb,),
        in_specs=in_specs,
        out_specs=pl.BlockSpec((ROW_BLOCK, d), lambda i: (jnp.maximum(i - 1, 0), 0)) if final else rows(d),
        out_shape=jax.ShapeDtypeStruct((t - ROW_BLOCK if final else t, d), F32),
        compiler_params=_params(("arbitrary",) if final else ("parallel",)),
        name="out_proj_final" if final else "out_proj",
    )(*args)


def _lru_kernel(*refs, reverse, nb):
    if reverse:
        (x_ref, xp_ref, xn_ref, cw_ref, cb_ref, lam_ref, wa_ref, ba_ref, wx_ref, bx_ref, hf_ref,
         o_ref, xs_sc, a_sc, b_sc, carry_sc) = refs
    else:
        (x_ref, xp_ref, xn_ref, cw_ref, cb_ref, lam_ref, wa_ref, ba_ref, wx_ref, bx_ref,
         o_ref, xs_sc, a_sc, b_sc, carry_sc) = refs
    step = pl.program_id(0)
    blk = jnp.where(step == 0, 0, nb - step) if reverse else step
    tb = x_ref.shape[0]

    @pl.when(step == 0)
    def _():
        carry_sc[...] = jnp.zeros_like(carry_sc)

    prev_ok = blk >= 2
    next_ok = (blk >= 1) & (blk < nb - 1)
    xs_sc[0:SUBLANES, :] = jnp.where(prev_ok, xp_ref[...], 0.0)
    xs_sc[SUBLANES:SUBLANES + tb, :] = x_ref[...]
    xs_sc[SUBLANES + tb:2 * SUBLANES + tb, :] = jnp.where(next_ok, xn_ref[...], 0.0)
    u = cb_ref[...]
    for j in range(CONV_W):
        off = SUBLANES + j - CONV_W // 2
        u = u + xs_sc[off:off + tb, :] * cw_ref[j:j + 1, :]

    ub = u.astype(BF16)
    r = jax.nn.sigmoid(_dot(ub, wa_ref[...]) + ba_ref[...])
    gi = jax.nn.sigmoid(_dot(ub, wx_ref[...]) + bx_ref[...])
    nl = -lam_ref[...]
    softplus = jnp.maximum(nl, 0.0) + jnp.log1p(jnp.exp(-jnp.abs(nl)))
    log_a = -LRU_C * r * softplus
    a_sc[...] = jnp.exp(log_a)
    b_sc[...] = jnp.sqrt(1.0 - jnp.exp(2.0 * log_a)) * (gi * u)

    row = lax.broadcasted_iota(jnp.int32, (SUBLANES, 1), 0)
    nchunk = tb // SUBLANES

    def chunk(c, carry):
        cc = nchunk - 1 - c if reverse else c
        start = pl.multiple_of(cc * SUBLANES, SUBLANES)
        a = a_sc[pl.ds(start, SUBLANES), :]
        b = b_sc[pl.ds(start, SUBLANES), :]
        for k in (1, 2, 4):
            shift = SUBLANES - k if reverse else k
            valid = (row < SUBLANES - k) if reverse else (row >= k)
            a_sh = pltpu.roll(a, shift, 0)
            b_sh = pltpu.roll(b, shift, 0)
            b = jnp.where(valid, a * b_sh + b, b)
            a = jnp.where(valid, a * a_sh, a)
        h = b + a * carry
        if reverse:
            o_ref[pl.ds(start, SUBLANES), :] = h + hf_ref[pl.ds(start, SUBLANES), :]
            new = h[0:1, :]
        else:
            o_ref[pl.ds(start, SUBLANES), :] = h
            new = h[SUBLANES - 1:SUBLANES, :]
        return jnp.broadcast_to(new, carry.shape)

    carry_sc[...] = lax.fori_loop(0, nchunk, chunk, carry_sc[...], unroll=True)


def _lru_pass(x, conv_w, conv_b, lam, wa, ba, wx, bx, hf, *, reverse):
    t, c = x.shape
    tb = ROW_BLOCK
    nb = t // tb
    per8 = tb // SUBLANES
    nb8 = t // SUBLANES
    if reverse:
        blk = lambda s: jnp.where(s == 0, 0, nb - s)
    else:
        blk = lambda s: s
    full = lambda a: pl.BlockSpec(a.shape, lambda s: (0,) * a.ndim)
    main = pl.BlockSpec((tb, c), lambda s: (blk(s), 0))
    args = [x, x, x, conv_w, conv_b, lam, wa, ba, wx, bx]
    in_specs = [main,
                pl.BlockSpec((SUBLANES, c), lambda s: (jnp.maximum(blk(s) * per8 - 1, 0), 0)),
                pl.BlockSpec((SUBLANES, c), lambda s: (jnp.minimum((blk(s) + 1) * per8, nb8 - 1), 0)),
                full(conv_w), full(conv_b), full(lam), full(wa), full(ba), full(wx), full(bx)]
    if reverse:
        args.append(hf)
        in_specs.append(main)
    return pl.pallas_call(
        functools.partial(_lru_kernel, reverse=reverse, nb=nb),
        grid=(nb,),
        in_specs=in_specs,
        out_specs=main,
        out_shape=jax.ShapeDtypeStruct((t, c), F32),
        scratch_shapes=[pltpu.VMEM((tb + 2 * SUBLANES, c), F32), pltpu.VMEM((tb, c), F32),
                        pltpu.VMEM((tb, c), F32), pltpu.VMEM((SUBLANES, c), F32)],
        compiler_params=_params(("arbitrary",)),
        name="lru_bwd" if reverse else "lru_fwd",
    )(*args)


def _block_diag(w):
    g, bw, _ = w.shape
    eye = jnp.eye(g, dtype=w.dtype)
    return (eye[:, None, :, None] * w[:, :, None, :]).reshape(g * bw, g * bw)


def _rope_tables(n, ctx_len, rot_dim, lead, tail, repeat):
    t = jnp.arange(n)
    row = (t // GRID_W).astype(F32)
    col = (t % GRID_W).astype(F32)
    n_freq = rot_dim // 4
    inv = ROPE_BASE ** (-jnp.arange(n_freq, dtype=F32) / n_freq)
    ang = jnp.concatenate([row[:, None] * inv, col[:, None] * inv], axis=-1)
    cos, sin = jnp.cos(ang), jnp.sin(ang)
    cos_l = jnp.concatenate([jnp.ones((n, lead), F32)] + [cos, cos] * repeat + [jnp.ones((n, tail), F32)], axis=-1)
    sin_l = jnp.concatenate([jnp.zeros((n, lead), F32)] + [-sin, sin] * repeat + [jnp.zeros((n, tail), F32)], axis=-1)
    cos_t = jnp.concatenate([jnp.ones((ctx_len, LANES), F32), cos_l], axis=0)
    sin_t = jnp.concatenate([jnp.zeros((ctx_len, LANES), F32), sin_l], axis=0)
    return cos_t, sin_t


def _even_weights(w_in, w_uq, w_ukv):
    d = w_in.shape[0]
    q, k, v, g_na, lq, lkv, kr, g_mla = jnp.split(
        w_in, [512, 1024, 1536, 2048, 2048 + Q_LORA, 2048 + Q_LORA + KV_LORA, 2048 + Q_LORA + KV_LORA + MLA_ROPE], axis=1)
    half = MLA_ROPE // 2
    pad = LANES - MLA_QK
    kr_pad = jnp.concatenate([jnp.zeros((d, MLA_NOPE), F32), kr, jnp.zeros((d, pad), F32)], axis=1)
    krs_pad = jnp.concatenate([jnp.zeros((d, MLA_NOPE), F32), kr[:, half:], kr[:, :half], jnp.zeros((d, pad), F32)], axis=1)
    w = jnp.concatenate([q * (NA_HD ** -0.5), k, v, g_na, g_mla, lq, lkv, kr_pad, krs_pad], axis=1).astype(BF16)
    uq = w_uq.reshape(Q_LORA, MLA_HEADS, MLA_QK)
    zq = jnp.zeros((Q_LORA, MLA_HEADS, pad), F32)
    uq_main = jnp.concatenate([uq, zq], axis=-1)
    uq_swap = jnp.concatenate([jnp.zeros((Q_LORA, MLA_HEADS, MLA_NOPE), F32), uq[..., MLA_NOPE + half:],
                               uq[..., MLA_NOPE:MLA_NOPE + half], zq], axis=-1)
    wuq = jnp.concatenate([uq_main.reshape(Q_LORA, -1), uq_swap.reshape(Q_LORA, -1)], axis=1).astype(BF16)
    ukv = w_ukv.reshape(KV_LORA, MLA_HEADS, MLA_NOPE + MLA_V)
    k_nope = jnp.concatenate([ukv[..., :MLA_NOPE], jnp.zeros((KV_LORA, MLA_HEADS, LANES - MLA_NOPE), F32)], axis=-1)
    wkv = jnp.concatenate([k_nope.reshape(KV_LORA, -1), ukv[..., MLA_NOPE:].reshape(KV_LORA, -1)], axis=1).astype(BF16)
    return w, wuq, wkv


def _odd_weights(w_in):
    d = w_in.shape[0]
    q, k, v, g_d, u, g_lru = jnp.split(w_in, [512, 1024, 1536, 2048, 2560], axis=1)

    def swap(m):
        return m.reshape(d, DIFF_HEADS, 2, 2, DIFF_HD // 2)[:, :, :, ::-1, :].reshape(d, DIFF_W)

    scale = DIFF_HD ** -0.5
    return jnp.concatenate([q * scale, swap(q) * scale, k, swap(k), v, g_d, g_lru, u], axis=1).astype(BF16)


def kernel(x, c, ctx, c_ctx, mod_w, mod_b, norm_g, final_g, e_w_in, e_w_out, na_rpb, mla_q_norm, mla_w_uq, mla_kv_norm, mla_w_ukv, o_w_in, o_w_out, diff_lq1, diff_lk1, diff_lq2, diff_lk2, diff_subln, lru_conv_w, lru_conv_b, lru_lambda, lru_wa, lru_ba, lru_wx, lru_bx):
    batch, n, d = x.shape
    ctx_len = ctx.shape[1]
    assert batch == 1 and ctx_len == ROW_BLOCK and d == D_MODEL
    assert n // GRID_W >= WIN_R

    cvec = jnp.concatenate([c_ctx[None], c, jnp.zeros((SUBLANES - 2, d), F32)], axis=0)
    mods = _modulation(cvec, mod_w, mod_b)[:, 0:2].reshape(DEPTH, 2, 1, 3 * d)

    cos_m, sin_m = _rope_tables(n, ctx_len, MLA_ROPE, MLA_NOPE, LANES - MLA_QK, 1)
    cos_d, sin_d = _rope_tables(n, ctx_len, DIFF_HD, 0, 0, 2)

    xs = jnp.concatenate([ctx[0], x[0]], axis=0)
    for l in range(DEPTH):
        i = l // 2
        g_row = norm_g[l][None]
        if l % 2 == 0:
            w, wuq, wkv = _even_weights(e_w_in[i], mla_w_uq[i], mla_w_ukv[i])
            naq, nak, nav, gate, mq, mk, mv = _in_even(
                xs, g_row, mods[l], w, mla_q_norm[i][None], wuq, mla_kv_norm[i][None], wkv, cos_m, sin_m)
            a = _na_attention(naq, nak, nav, _na_bias_table(na_rpb[i]))
            b = _flash(mq, mk, mv, (), diff=False)
            w_out = e_w_out[i]
        else:
            lam_init = 0.8 - 0.6 * math.exp(-0.3 * l)
            dq, dk, dv, gate, u = _in_odd(xs, g_row, mods[l], _odd_weights(o_w_in[i]), cos_d, sin_d)
            lamv = jnp.stack([diff_lq1[i], diff_lk1[i], diff_lq2[i], diff_lk2[i]])
            a = _flash(dq, dk, dv, (lamv, diff_subln[i][None]), diff=True, lam_init=lam_init)
            lru = lambda dr, hf: _lru_pass(
                u, lru_conv_w[i], lru_conv_b[i][None], lru_lambda[i, dr][None],
                _block_diag(lru_wa[i, dr]).astype(BF16), lru_ba[i, dr][None],
                _block_diag(lru_wx[i, dr]).astype(BF16), lru_bx[i, dr][None], hf, reverse=dr == 1)
            b = lru(1, lru(0, None))
            w_out = o_w_out[i]
        xs = _out_proj(a, b, gate, w_out.astype(BF16), xs, mods[l], final_g[None] if l == DEPTH - 1 else None)
    return xs[None]
```

```python
import functools
import math

import jax
import jax.numpy as jnp
import numpy as np
from jax import lax
from jax.experimental import pallas as pl
from jax.experimental.pallas import tpu as pltpu

F32 = jnp.float32
BF16 = jnp.bfloat16

D_MODEL = 1024
DEPTH = 4
GRID_W = 64
EPS = 1e-6
ROPE_BASE = 10000.0

NA_HEADS = 8
NA_HD = 64
NA_W = NA_HEADS * NA_HD
WIN_R = 8
WIN_C = 16

MLA_HEADS = 8
MLA_NOPE = 64
MLA_ROPE = 32
MLA_V = 64
MLA_QK = MLA_NOPE + MLA_ROPE
MLA_W = MLA_HEADS * MLA_V
Q_LORA = 384
KV_LORA = 256

DIFF_HEADS = 4
DIFF_HD = 64
DIFF_W = DIFF_HEADS * 2 * DIFF_HD

LRU_W = 512
LRU_BLOCKS = 8
LRU_BW = LRU_W // LRU_BLOCKS
CONV_W = 4
LRU_C = 8.0

LANES = 128
SUBLANES = 8
ROW_BLOCK = 256
KV_CHUNK = 1024
SOFTMAX_ROWS = 256
VT_ROWS = LANES + 16
VMEM_LIMIT = 56 * 1024 * 1024
NEG_BIG = -1e30
LOG2E = math.log2(math.e)

_NT = (((1,), (1,)), ((), ()))


def _dot_nt(a, b):
    return lax.dot_general(a, b, _NT, preferred_element_type=F32)


def _dot(a, b):
    return jnp.dot(a, b, preferred_element_type=F32)


def _params(sem):
    return pltpu.CompilerParams(dimension_semantics=sem, vmem_limit_bytes=VMEM_LIMIT)


def _mod_kernel(c_ref, w_ref, b_ref, o_ref):
    c = c_ref[...]
    s = c * jax.nn.sigmoid(c)
    o_ref[...] = _dot(s.astype(BF16), w_ref[...].astype(BF16)) + b_ref[...]


def _modulation(cvec, mod_w, mod_b):
    depth, d, d3 = mod_w.shape
    tn = 512
    return pl.pallas_call(
        _mod_kernel,
        grid=(depth, d3 // tn),
        in_specs=[
            pl.BlockSpec((SUBLANES, d), lambda l, j: (0, 0)),
            pl.BlockSpec((None, d, tn), lambda l, j: (l, 0, j)),
            pl.BlockSpec((None, 1, tn), lambda l, j: (l, 0, j)),
        ],
        out_specs=pl.BlockSpec((None, SUBLANES, tn), lambda l, j: (l, 0, j)),
        out_shape=jax.ShapeDtypeStruct((depth, SUBLANES, d3), F32),
        compiler_params=_params(("parallel", "parallel")),
        name="modulation",
    )(cvec, mod_w, mod_b.reshape(depth, 1, d3))


def _norm_mod(x, g, mod):
    d = x.shape[-1]
    y = x * lax.rsqrt(jnp.mean(x * x, axis=-1, keepdims=True) + EPS) * g
    return y * (1.0 + mod[:, d:2 * d]) + mod[:, 0:d]


def _rms(x, g):
    return x * lax.rsqrt(jnp.mean(x * x, axis=-1, keepdims=True) + EPS) * g


E_Q, E_K, E_V, E_LQ, E_LKV, E_KR, E_KRS, E_END = 0, 512, 1024, 1536, 1920, 2176, 2304, 2432


def _in_even_kernel(x_ref, g_ref, mod_ref, w_ref, qn_ref, wuq_ref, kvn_ref, wkv_ref, c_ref, s_ref,
                    naq_ref, nak_ref, nav_ref, mq_ref, mk_ref, mv_ref):
    hb = _norm_mod(x_ref[...], g_ref[...], mod_ref[...]).astype(BF16)
    npair = NA_HEADS // 2
    for ref, off in ((naq_ref, E_Q), (nak_ref, E_K), (nav_ref, E_V)):
        seg = _dot(hb, w_ref[:, off:off + NA_W])
        for p in range(npair):
            ref[p] = seg[:, p * LANES:(p + 1) * LANES].astype(BF16)
    cos = c_ref[...]
    sin = s_ref[...]
    qn = _rms(_dot(hb, w_ref[:, E_LQ:E_LKV]), qn_ref[...]).astype(BF16)
    width = MLA_HEADS * LANES
    q_main = _dot(qn, wuq_ref[:, 0:width])
    q_swap = _dot(qn, wuq_ref[:, width:2 * width])
    scale = MLA_QK ** -0.5 * LOG2E
    for h in range(MLA_HEADS):
        sl = slice(h * LANES, (h + 1) * LANES)
        mq_ref[h] = ((q_main[:, sl] * cos + q_swap[:, sl] * sin) * scale).astype(BF16)
    kvn = _rms(_dot(hb, w_ref[:, E_LKV:E_KR]), kvn_ref[...]).astype(BF16)
    kr = _dot(hb, w_ref[:, E_KR:E_KRS]) * cos + _dot(hb, w_ref[:, E_KRS:E_END]) * sin
    k_nope = _dot(kvn, wkv_ref[:, 0:width])
    for h in range(MLA_HEADS):
        mk_ref[h] = (k_nope[:, h * LANES:(h + 1) * LANES] + kr).astype(BF16)
    v = _dot(kvn, wkv_ref[:, width:width + MLA_W])
    for p in range(MLA_HEADS // 2):
        mv_ref[p, 0:LANES, :] = v[:, p * LANES:(p + 1) * LANES].T.astype(BF16)
    mv_ref[:, LANES:, :] = jnp.ones((MLA_HEADS // 2, VT_ROWS - LANES, ROW_BLOCK), BF16)


def _in_even(x, norm_g, mod, w, qn, wuq, kvn, wkv, cos, sin):
    t, d = x.shape
    nb = t // ROW_BLOCK
    full = lambda a: pl.BlockSpec(a.shape, lambda i: (0,) * a.ndim)
    rows = lambda c: pl.BlockSpec((ROW_BLOCK, c), lambda i: (i, 0))
    heads = lambda n: pl.BlockSpec((n, ROW_BLOCK, LANES), lambda i: (0, i, 0))
    hshape = lambda n: jax.ShapeDtypeStruct((n, t, LANES), BF16)
    return pl.pallas_call(
        _in_even_kernel,
        grid=(nb,),
        in_specs=[rows(d), full(norm_g),
                  pl.BlockSpec((None, 1, 3 * d), lambda i: (jnp.minimum(i, 1), 0, 0)),
                  full(w), full(qn), full(wuq), full(kvn), full(wkv), rows(LANES), rows(LANES)],
        out_specs=[heads(4), heads(4), heads(4), heads(8), heads(8),
                   pl.BlockSpec((4, VT_ROWS, ROW_BLOCK), lambda i: (0, 0, i))],
        out_shape=[hshape(4), hshape(4), hshape(4), hshape(8), hshape(8), jax.ShapeDtypeStruct((4, VT_ROWS, t), BF16)],
        compiler_params=_params(("parallel",)),
        name="in_proj_even",
    )(x, norm_g, mod, w, qn, wuq, kvn, wkv, cos, sin)


O_Q, O_QS, O_K, O_KS, O_V, O_U, O_END = 0, 512, 1024, 1536, 2048, 2560, 3072


def _in_odd_kernel(x_ref, g_ref, mod_ref, w_ref, c_ref, s_ref, dq_ref, dk_ref, dv_ref, u_ref):
    hb = _norm_mod(x_ref[...], g_ref[...], mod_ref[...]).astype(BF16)
    cos = c_ref[...]
    sin = s_ref[...]
    lane = lax.broadcasted_iota(jnp.int32, (1, LANES), 1)
    first = lane < DIFF_HD
    q = _dot(hb, w_ref[:, O_Q:O_QS])
    qs = _dot(hb, w_ref[:, O_QS:O_K])
    k = _dot(hb, w_ref[:, O_K:O_KS])
    ks = _dot(hb, w_ref[:, O_KS:O_V])
    v = _dot(hb, w_ref[:, O_V:O_U])
    for h in range(DIFF_HEADS):
        sl = slice(h * LANES, (h + 1) * LANES)
        qr = (q[:, sl] * cos + qs[:, sl] * sin) * LOG2E
        dq_ref[h, 0] = jnp.where(first, qr, 0.0).astype(BF16)
        dq_ref[h, 1] = jnp.where(first, 0.0, qr).astype(BF16)
        dk_ref[h] = (k[:, sl] * cos + ks[:, sl] * sin).astype(BF16)
        dv_ref[h, 0:LANES, :] = v[:, sl].T.astype(BF16)
    dv_ref[:, LANES:, :] = jnp.ones((DIFF_HEADS, VT_ROWS - LANES, ROW_BLOCK), BF16)
    u_ref[...] = _dot(hb, w_ref[:, O_U:O_END])


def _in_odd(x, norm_g, mod, w, cos, sin):
    t, d = x.shape
    nb = t // ROW_BLOCK
    full = lambda a: pl.BlockSpec(a.shape, lambda i: (0,) * a.ndim)
    rows = lambda c: pl.BlockSpec((ROW_BLOCK, c), lambda i: (i, 0))
    heads = pl.BlockSpec((DIFF_HEADS, ROW_BLOCK, LANES), lambda i: (0, i, 0))
    hshape = jax.ShapeDtypeStruct((DIFF_HEADS, t, LANES), BF16)
    return pl.pallas_call(
        _in_odd_kernel,
        grid=(nb,),
        in_specs=[rows(d), full(norm_g),
                  pl.BlockSpec((None, 1, 3 * d), lambda i: (jnp.minimum(i, 1), 0, 0)),
                  full(w), rows(LANES), rows(LANES)],
        out_specs=[pl.BlockSpec((DIFF_HEADS, 2, ROW_BLOCK, LANES), lambda i: (0, 0, i, 0)),
                   heads, pl.BlockSpec((DIFF_HEADS, VT_ROWS, ROW_BLOCK), lambda i: (0, 0, i)),
                   rows(LRU_W)],
        out_shape=[jax.ShapeDtypeStruct((DIFF_HEADS, 2, t, LANES), BF16), hshape,
                   jax.ShapeDtypeStruct((DIFF_HEADS, VT_ROWS, t), BF16),
                   jax.ShapeDtypeStruct((t, LRU_W), F32)],
        compiler_params=_params(("parallel",)),
        name="in_proj_odd",
    )(x, norm_g, mod, w, cos, sin)


def _sublane_all(op, x):
    for k in (4, 2, 1):
        x = op(x, pltpu.roll(x, k, 0))
    return x


def _flash_kernel(*refs, diff, lam_init):
    if diff:
        q_ref, qn_ref, k_ref, vt_ref, lamv_ref, subln_ref, o_ref = refs[:7]
    else:
        q_ref, qn_ref, k_ref, vt_ref, o_ref = refs[:5]
    m_sc, acc_sc, s0_sc, s1_sc, p0_sc, p1_sc, a0_sc, a1_sc, c0_sc, c1_sc = refs[-10:]
    i = pl.program_id(1)
    tq = q_ref.shape[1]
    dv = vt_ref.shape[0]
    t = vt_ref.shape[1]

    def scores(s_sc, c_sc, start, size, queries=q_ref):
        for s in range(2):
            kc = k_ref[pl.ds(start, size), :] if diff else k_ref[s, pl.ds(start, size), :]
            st = _dot_nt(kc, queries[s])
            s_sc[s, 0:size, :] = st
            c_sc[s] = jnp.max(st.reshape(size // SUBLANES, SUBLANES, tq), axis=0)

    def softmax(s_sc, c_sc, p_sc, a_sc, size):
        for s in range(2):
            m_prev = m_sc[s]
            m_new = jnp.maximum(m_prev, _sublane_all(jnp.maximum, c_sc[s]))
            alpha = jnp.exp2(m_prev - m_new)
            for r in range(0, size, SOFTMAX_ROWS):
                x = s_sc[s, r:r + SOFTMAX_ROWS, :].reshape(SOFTMAX_ROWS // SUBLANES, SUBLANES, tq) - m_new[None]
                p = jnp.exp2(x)
                p_sc[s, r:r + SOFTMAX_ROWS, :] = p.reshape(SOFTMAX_ROWS, tq).astype(BF16)
            a_sc[s] = alpha
            m_sc[s] = m_new

    def accumulate(p_sc, a_sc, start, size):
        vt = vt_ref[:, pl.ds(start, size)]
        for s in range(2):
            acc = acc_sc[s].reshape(dv // SUBLANES, SUBLANES, tq) * a_sc[s][None]
            acc_sc[s] = acc.reshape(dv, tq) + _dot(vt, p_sc[s, 0:size, :])

    bufs = ((s0_sc, c0_sc, p0_sc, a0_sc), (s1_sc, c1_sc, p1_sc, a1_sc))
    n_chunks = (t - ROW_BLOCK) // KV_CHUNK

    def result(s):
        l = acc_sc[s, LANES:LANES + SUBLANES, :]
        o_t = acc_sc[s, 0:LANES, :].reshape(LANES // SUBLANES, SUBLANES, tq) / l[None]
        return o_t.reshape(LANES, tq).T

    def finish():
        o0 = result(0)
        o1 = result(1)
        if diff:
            lv = lamv_ref[...]
            lam = (jnp.exp(jnp.sum(lv[0:1] * lv[1:2], axis=-1, keepdims=True))
                   - jnp.exp(jnp.sum(lv[2:3] * lv[3:4], axis=-1, keepdims=True)) + lam_init)
            o = o0 - lam * o1
            o_ref[...] = _rms(o, subln_ref[...]) * (1.0 - lam_init)
        else:
            lane = lax.broadcasted_iota(jnp.int32, (1, LANES), 1)
            o_ref[...] = jnp.where(lane < MLA_V, o0, o1)

    m_sc[...] = jnp.full(m_sc.shape, NEG_BIG, F32)
    acc_sc[...] = jnp.zeros(acc_sc.shape, F32)

    @pl.when(i == 0)
    def _():
        scores(s0_sc, c0_sc, 0, ROW_BLOCK)
        softmax(s0_sc, c0_sc, p0_sc, a0_sc, ROW_BLOCK)
        scores(s1_sc, c1_sc, ROW_BLOCK, KV_CHUNK, qn_ref)
        accumulate(p0_sc, a0_sc, 0, ROW_BLOCK)
        finish()

    @pl.when(i > 0)
    def _():
        for n in range(1, n_chunks + 1):
            s_a, c_a, p_a, a_a = bufs[(n - 1) % 2]
            s_b, c_b, p_b, a_b = bufs[n % 2]
            if n < n_chunks:
                scores(s_a, c_a, ROW_BLOCK + n * KV_CHUNK, KV_CHUNK)
            else:
                scores(s_a, c_a, 0, ROW_BLOCK)
            if n > 1:
                accumulate(p_a, a_a, ROW_BLOCK + (n - 2) * KV_CHUNK, KV_CHUNK)
            softmax(s_b, c_b, p_b, a_b, KV_CHUNK)
        s_a, c_a, p_a, a_a = bufs[n_chunks % 2]
        s_b, c_b, p_b, a_b = bufs[(n_chunks + 1) % 2]
        accumulate(p_a, a_a, t - KV_CHUNK, KV_CHUNK)
        softmax(s_b, c_b, p_b, a_b, ROW_BLOCK)
        scores(s1_sc, c1_sc, ROW_BLOCK, KV_CHUNK, qn_ref)
        accumulate(p_b, a_b, 0, ROW_BLOCK)
        finish()


def _flash(q, k, vt, extra, *, diff, lam_init=0.0):
    groups, t = vt.shape[0], vt.shape[2]
    nb = t // ROW_BLOCK
    assert (t - ROW_BLOCK) % (2 * KV_CHUNK) == 0
    if diff:
        q_spec = pl.BlockSpec((None, 2, ROW_BLOCK, LANES), lambda g, i: (g, 0, i, 0))
        qn_spec = pl.BlockSpec((None, 2, ROW_BLOCK, LANES), lambda g, i: (g, 0, jnp.minimum(i + 1, nb - 1), 0))
        k_spec = pl.BlockSpec((None, t, LANES), lambda g, i: (g, 0, 0))
        extra_specs = [pl.BlockSpec(e.shape, lambda g, i: (0, 0)) for e in extra]
    else:
        q_spec = pl.BlockSpec((2, ROW_BLOCK, LANES), lambda g, i: (g, i, 0))
        qn_spec = pl.BlockSpec((2, ROW_BLOCK, LANES), lambda g, i: (g, jnp.minimum(i + 1, nb - 1), 0))
        k_spec = pl.BlockSpec((2, t, LANES), lambda g, i: (g, 0, 0))
        extra_specs = []
    return pl.pallas_call(
        functools.partial(_flash_kernel, diff=diff, lam_init=lam_init),
        grid=(groups, nb),
        in_specs=[q_spec, qn_spec, k_spec, pl.BlockSpec((None, VT_ROWS, t), lambda g, i: (g, 0, 0))] + extra_specs,
        out_specs=pl.BlockSpec((ROW_BLOCK, LANES), lambda g, i: (i, g)),
        out_shape=jax.ShapeDtypeStruct((t, groups * LANES), F32),
        scratch_shapes=[pltpu.VMEM((2, SUBLANES, ROW_BLOCK), F32), pltpu.VMEM((2, VT_ROWS, ROW_BLOCK), F32)]
        + [pltpu.VMEM((2, KV_CHUNK, ROW_BLOCK), F32)] * 2 + [pltpu.VMEM((2, KV_CHUNK, ROW_BLOCK), BF16)] * 2
        + [pltpu.VMEM((2, SUBLANES, ROW_BLOCK), F32)] * 4,
        compiler_params=_params(("parallel", "arbitrary")),
        name="flash_diff" if diff else "flash_mla",
    )(q, q, k, vt, *extra)


NA_ROWS_PER_STEP = ROW_BLOCK // GRID_W


def _na_kernel(q_ref, k_ref, v_ref, bt_ref, o_ref, *, grid_rows):
    i = pl.program_id(1)
    lane = lax.broadcasted_iota(jnp.int32, (1, LANES), 1)
    first = lane < NA_HD
    kctx = k_ref[0:ROW_BLOCK, :]
    vctx = v_ref[0:ROW_BLOCK, :]

    def stacked(q):
        zero = jnp.zeros_like(q)
        return jnp.concatenate([jnp.where(first, q, zero), jnp.where(first, zero, q)], axis=0)

    @pl.when(i == 0)
    def _():
        q2 = stacked(q_ref[...])
        sc = _dot_nt(q2, kctx)
        p = jnp.exp(sc - jnp.max(sc, axis=-1, keepdims=True))
        o = _dot(p.astype(BF16), vctx) / jnp.sum(p, axis=-1, keepdims=True)
        o_ref[...] = jnp.where(first, o[0:ROW_BLOCK], o[ROW_BLOCK:])

    @pl.when(i > 0)
    def _():
        nkeys = WIN_R * GRID_W
        starts, s_lat, s_ctx = [], [], []
        for rr in range(NA_ROWS_PER_STEP):
            r = (i - 1) * NA_ROWS_PER_STEP + rr
            r0 = jnp.clip(r - WIN_R // 2, 0, grid_rows - WIN_R)
            start = pl.multiple_of(ROW_BLOCK + r0 * GRID_W, GRID_W)
            q2 = stacked(q_ref[rr * GRID_W:(rr + 1) * GRID_W, :])
            starts.append(start)
            s_lat.append(_dot_nt(q2, k_ref[pl.ds(start, nkeys), :]) + bt_ref[r0 - r + WIN_R - 1])
            s_ctx.append(_dot_nt(q2, kctx))
        p_lat, p_ctx, denom = [], [], []
        for rr in range(NA_ROWS_PER_STEP):
            m = jnp.maximum(jnp.max(s_lat[rr], axis=-1, keepdims=True), jnp.max(s_ctx[rr], axis=-1, keepdims=True))
            pl_, pc_ = jnp.exp(s_lat[rr] - m), jnp.exp(s_ctx[rr] - m)
            denom.append(jnp.sum(pl_, axis=-1, keepdims=True) + jnp.sum(pc_, axis=-1, keepdims=True))
            p_lat.append(pl_.astype(BF16))
            p_ctx.append(pc_.astype(BF16))
        for rr in range(NA_ROWS_PER_STEP):
            o = (_dot(p_lat[rr], v_ref[pl.ds(starts[rr], nkeys), :]) + _dot(p_ctx[rr], vctx)) / denom[rr]
            o_ref[rr * GRID_W:(rr + 1) * GRID_W, :] = jnp.where(first, o[0:GRID_W], o[GRID_W:])


def _na_attention(q, k, v, bias):
    npair, t = q.shape[0], q.shape[1]
    nb = t // ROW_BLOCK
    grid_rows = (t - ROW_BLOCK) // GRID_W
    resident = pl.BlockSpec((None, t, LANES), lambda p, i: (p, 0, 0))
    return pl.pallas_call(
        functools.partial(_na_kernel, grid_rows=grid_rows),
        grid=(npair, nb),
        in_specs=[pl.BlockSpec((None, ROW_BLOCK, LANES), lambda p, i: (p, i, 0)), resident, resident,
                  pl.BlockSpec((None,) + bias.shape[1:], lambda p, i: (p, 0, 0, 0))],
        out_specs=pl.BlockSpec((ROW_BLOCK, LANES), lambda p, i: (i, p)),
        out_shape=jax.ShapeDtypeStruct((t, npair * LANES), F32),
        compiler_params=_params(("parallel", "arbitrary")),
        name="na_attention",
    )(q, k, v, bias)


def _na_bias_table(rpb):
    w = np.arange(GRID_W)
    c0 = np.clip(w - WIN_C // 2, 0, GRID_W - WIN_C)
    inside = (w[None, :] >= c0[:, None]) & (w[None, :] < c0[:, None] + WIN_C)
    edge = GRID_W - WIN_C
    pad = jnp.pad(rpb, ((0, 0), (0, 0), (edge, edge)))
    col = jnp.stack([pad[:, :, GRID_W - 1 - wq:2 * GRID_W - 1 - wq] for wq in range(GRID_W)], axis=2)
    col = jnp.where(inside[None, None], col, NEG_BIG)
    tab = jnp.stack([col[:, var:var + WIN_R] for var in range(WIN_R)], axis=1)
    tab = tab.transpose(0, 1, 3, 2, 4).reshape(NA_HEADS // 2, 2, WIN_R, GRID_W, WIN_R * GRID_W)
    return tab.transpose(0, 2, 1, 3, 4).reshape(NA_HEADS // 2, WIN_R, 2 * GRID_W, WIN_R * GRID_W)


def _out_kernel(*refs, final):
    if final:
        a_ref, b_ref, g_ref, wg_ref, w_ref, x_ref, mod_ref, fg_ref, o_ref = refs
    else:
        a_ref, b_ref, g_ref, wg_ref, w_ref, x_ref, mod_ref, o_ref = refs
    half = a_ref.shape[-1]
    d = x_ref.shape[-1]
    hb = _norm_mod(x_ref[...], g_ref[...], mod_ref[...]).astype(BF16)
    ga = _dot(hb, wg_ref[:, 0:half])
    gb = _dot(hb, wg_ref[:, half:2 * half])
    ya = (a_ref[...] * (ga * jax.nn.sigmoid(ga))).astype(BF16)
    yb = (b_ref[...] * (gb * jax.nn.sigmoid(gb))).astype(BF16)
    y = _dot(ya, w_ref[0:half, :]) + _dot(yb, w_ref[half:2 * half, :])
    x_new = x_ref[...] + mod_ref[:, 2 * d:3 * d] * y
    o_ref[...] = _rms(x_new, fg_ref[...]) if final else x_new


def _out_proj(a, b, norm_g, w_gate, w, x, mod, final_g=None):
    t, d = x.shape
    nb = t // ROW_BLOCK
    final = final_g is not None
    rows = lambda c: pl.BlockSpec((ROW_BLOCK, c), lambda i: (i, 0))
    in_specs = [rows(a.shape[1]), rows(b.shape[1]), pl.BlockSpec(norm_g.shape, lambda i: (0, 0)),
                pl.BlockSpec(w_gate.shape, lambda i: (0, 0)), pl.BlockSpec(w.shape, lambda i: (0, 0)), rows(d),
                pl.BlockSpec((None, 1, 3 * d), lambda i: (jnp.minimum(i, 1), 0, 0))]
    args = [a, b, norm_g, w_gate, w, x, mod]
    if final:
        in_specs.append(pl.BlockSpec((1, d), lambda i: (0, 0)))
        args.append(final_g)
    return pl.pallas_call(
        functools.partial(_out_kernel, final=final),
        grid=(nb,),
        in_specs=in_specs,
        out_specs=pl.BlockSpec((ROW_BLOCK, d), lambda i: (jnp.maximum(i - 1, 0), 0)) if final else rows(d),
        out_shape=jax.ShapeDtypeStruct((t - ROW_BLOCK if final else t, d), F32),
        compiler_params=_params(("arbitrary",) if final else ("parallel",)),
        name="out_proj_final" if final else "out_proj",
    )(*args)


def _lru_kernel(*refs, reverse, nb):
    if reverse:
        (x_ref, xp_ref, xn_ref, cw_ref, cb_ref, lam_ref, wa_ref, ba_ref, wx_ref, bx_ref, hf_ref,
         o_ref, xs_sc, a_sc, b_sc, carry_sc) = refs
    else:
        (x_ref, xp_ref, xn_ref, cw_ref, cb_ref, lam_ref, wa_ref, ba_ref, wx_ref, bx_ref,
         o_ref, xs_sc, a_sc, b_sc, carry_sc) = refs
    step = pl.program_id(0)
    blk = jnp.where(step == 0, 0, nb - step) if reverse else step
    tb = x_ref.shape[0]

    @pl.when(step == 0)
    def _():
        carry_sc[...] = jnp.zeros_like(carry_sc)

    prev_ok = blk >= 2
    next_ok = (blk >= 1) & (blk < nb - 1)
    xs_sc[0:SUBLANES, :] = jnp.where(prev_ok, xp_ref[...], 0.0)
    xs_sc[SUBLANES:SUBLANES + tb, :] = x_ref[...]
    xs_sc[SUBLANES + tb:2 * SUBLANES + tb, :] = jnp.where(next_ok, xn_ref[...], 0.0)
    u = cb_ref[...]
    for j in range(CONV_W):
        off = SUBLANES + j - CONV_W // 2
        u = u + xs_sc[off:off + tb, :] * cw_ref[j:j + 1, :]

    ub = u.astype(BF16)
    r = jax.nn.sigmoid(_dot(ub, wa_ref[...]) + ba_ref[...])
    gi = jax.nn.sigmoid(_dot(ub, wx_ref[...]) + bx_ref[...])
    nl = -lam_ref[...]
    softplus = jnp.maximum(nl, 0.0) + jnp.log1p(jnp.exp(-jnp.abs(nl)))
    log_a = -LRU_C * r * softplus
    a_sc[...] = jnp.exp(log_a)
    b_sc[...] = jnp.sqrt(1.0 - jnp.exp(2.0 * log_a)) * (gi * u)

    row = lax.broadcasted_iota(jnp.int32, (SUBLANES, 1), 0)
    nchunk = tb // SUBLANES

    def chunk(c, carry):
        cc = nchunk - 1 - c if reverse else c
        start = pl.multiple_of(cc * SUBLANES, SUBLANES)
        a = a_sc[pl.ds(start, SUBLANES), :]
        b = b_sc[pl.ds(start, SUBLANES), :]
        for k in (1, 2, 4):
            shift = SUBLANES - k if reverse else k
            valid = (row < SUBLANES - k) if reverse else (row >= k)
            a_sh = pltpu.roll(a, shift, 0)
            b_sh = pltpu.roll(b, shift, 0)
            b = jnp.where(valid, a * b_sh + b, b)
            a = jnp.where(valid, a * a_sh, a)
        h = b + a * carry
        if reverse:
            o_ref[pl.ds(start, SUBLANES), :] = h + hf_ref[pl.ds(start, SUBLANES), :]
            new = h[0:1, :]
        else:
            o_ref[pl.ds(start, SUBLANES), :] = h
            new = h[SUBLANES - 1:SUBLANES, :]
        return jnp.broadcast_to(new, carry.shape)

    carry_sc[...] = lax.fori_loop(0, nchunk, chunk, carry_sc[...], unroll=True)


def _lru_pass(x, conv_w, conv_b, lam, wa, ba, wx, bx, hf, *, reverse):
    t, c = x.shape
    tb = ROW_BLOCK
    nb = t // tb
    per8 = tb // SUBLANES
    nb8 = t // SUBLANES
    if reverse:
        blk = lambda s: jnp.where(s == 0, 0, nb - s)
    else:
        blk = lambda s: s
    full = lambda a: pl.BlockSpec(a.shape, lambda s: (0,) * a.ndim)
    main = pl.BlockSpec((tb, c), lambda s: (blk(s), 0))
    args = [x, x, x, conv_w, conv_b, lam, wa, ba, wx, bx]
    in_specs = [main,
                pl.BlockSpec((SUBLANES, c), lambda s: (jnp.maximum(blk(s) * per8 - 1, 0), 0)),
                pl.BlockSpec((SUBLANES, c), lambda s: (jnp.minimum((blk(s) + 1) * per8, nb8 - 1), 0)),
                full(conv_w), full(conv_b), full(lam), full(wa), full(ba), full(wx), full(bx)]
    if reverse:
        args.append(hf)
        in_specs.append(main)
    return pl.pallas_call(
        functools.partial(_lru_kernel, reverse=reverse, nb=nb),
        grid=(nb,),
        in_specs=in_specs,
        out_specs=main,
        out_shape=jax.ShapeDtypeStruct((t, c), F32),
        scratch_shapes=[pltpu.VMEM((tb + 2 * SUBLANES, c), F32), pltpu.VMEM((tb, c), F32),
                        pltpu.VMEM((tb, c), F32), pltpu.VMEM((SUBLANES, c), F32)],
        compiler_params=_params(("arbitrary",)),
        name="lru_bwd" if reverse else "lru_fwd",
    )(*args)


def _block_diag(w):
    g, bw, _ = w.shape
    eye = jnp.eye(g, dtype=w.dtype)
    return (eye[:, None, :, None] * w[:, :, None, :]).reshape(g * bw, g * bw)


def _rope_tables(n, ctx_len, rot_dim, lead, tail, repeat):
    t = jnp.arange(n)
    row = (t // GRID_W).astype(F32)
    col = (t % GRID_W).astype(F32)
    n_freq = rot_dim // 4
    inv = ROPE_BASE ** (-jnp.arange(n_freq, dtype=F32) / n_freq)
    ang = jnp.concatenate([row[:, None] * inv, col[:, None] * inv], axis=-1)
    cos, sin = jnp.cos(ang), jnp.sin(ang)
    cos_l = jnp.concatenate([jnp.ones((n, lead), F32)] + [cos, cos] * repeat + [jnp.ones((n, tail), F32)], axis=-1)
    sin_l = jnp.concatenate([jnp.zeros((n, lead), F32)] + [-sin, sin] * repeat + [jnp.zeros((n, tail), F32)], axis=-1)
    cos_t = jnp.concatenate([jnp.ones((ctx_len, LANES), F32), cos_l], axis=0)
    sin_t = jnp.concatenate([jnp.zeros((ctx_len, LANES), F32), sin_l], axis=0)
    return cos_t, sin_t


def _even_weights(w_in, w_uq, w_ukv):
    d = w_in.shape[0]
    q, k, v, g_na, lq, lkv, kr, g_mla = jnp.split(
        w_in, [512, 1024, 1536, 2048, 2048 + Q_LORA, 2048 + Q_LORA + KV_LORA, 2048 + Q_LORA + KV_LORA + MLA_ROPE], axis=1)
    half = MLA_ROPE // 2
    pad = LANES - MLA_QK
    kr_pad = jnp.concatenate([jnp.zeros((d, MLA_NOPE), F32), kr, jnp.zeros((d, pad), F32)], axis=1)
    krs_pad = jnp.concatenate([jnp.zeros((d, MLA_NOPE), F32), kr[:, half:], kr[:, :half], jnp.zeros((d, pad), F32)], axis=1)
    w = jnp.concatenate([q * (NA_HD ** -0.5), k, v, lq, lkv, kr_pad, krs_pad], axis=1).astype(BF16)
    w_gate = jnp.concatenate([g_na, g_mla], axis=1).astype(BF16)
    uq = w_uq.reshape(Q_LORA, MLA_HEADS, MLA_QK)
    zq = jnp.zeros((Q_LORA, MLA_HEADS, pad), F32)
    uq_main = jnp.concatenate([uq, zq], axis=-1)
    uq_swap = jnp.concatenate([jnp.zeros((Q_LORA, MLA_HEADS, MLA_NOPE), F32), uq[..., MLA_NOPE + half:],
                               uq[..., MLA_NOPE:MLA_NOPE + half], zq], axis=-1)
    wuq = jnp.concatenate([uq_main.reshape(Q_LORA, -1), uq_swap.reshape(Q_LORA, -1)], axis=1).astype(BF16)
    ukv = w_ukv.reshape(KV_LORA, MLA_HEADS, MLA_NOPE + MLA_V)
    k_nope = jnp.concatenate([ukv[..., :MLA_NOPE], jnp.zeros((KV_LORA, MLA_HEADS, LANES - MLA_NOPE), F32)], axis=-1)
    wkv = jnp.concatenate([k_nope.reshape(KV_LORA, -1), ukv[..., MLA_NOPE:].reshape(KV_LORA, -1)], axis=1).astype(BF16)
    return w, w_gate, wuq, wkv


def _odd_weights(w_in):
    d = w_in.shape[0]
    q, k, v, g_d, u, g_lru = jnp.split(w_in, [512, 1024, 1536, 2048, 2560], axis=1)

    def swap(m):
        return m.reshape(d, DIFF_HEADS, 2, 2, DIFF_HD // 2)[:, :, :, ::-1, :].reshape(d, DIFF_W)

    scale = DIFF_HD ** -0.5
    w = jnp.concatenate([q * scale, swap(q) * scale, k, swap(k), v, u], axis=1).astype(BF16)
    return w, jnp.concatenate([g_d, g_lru], axis=1).astype(BF16)


def kernel(x, c, ctx, c_ctx, mod_w, mod_b, norm_g, final_g, e_w_in, e_w_out, na_rpb, mla_q_norm, mla_w_uq, mla_kv_norm, mla_w_ukv, o_w_in, o_w_out, diff_lq1, diff_lk1, diff_lq2, diff_lk2, diff_subln, lru_conv_w, lru_conv_b, lru_lambda, lru_wa, lru_ba, lru_wx, lru_bx):
    batch, n, d = x.shape
    ctx_len = ctx.shape[1]
    assert batch == 1 and ctx_len == ROW_BLOCK and d == D_MODEL
    assert n // GRID_W >= WIN_R

    cvec = jnp.concatenate([c_ctx[None], c, jnp.zeros((SUBLANES - 2, d), F32)], axis=0)
    mods = _modulation(cvec, mod_w, mod_b)[:, 0:2].reshape(DEPTH, 2, 1, 3 * d)

    cos_m, sin_m = _rope_tables(n, ctx_len, MLA_ROPE, MLA_NOPE, LANES - MLA_QK, 1)
    cos_d, sin_d = _rope_tables(n, ctx_len, DIFF_HD, 0, 0, 2)

    xs = jnp.concatenate([ctx[0], x[0]], axis=0)
    for l in range(DEPTH):
        i = l // 2
        g_row = norm_g[l][None]
        if l % 2 == 0:
            w, w_gate, wuq, wkv = _even_weights(e_w_in[i], mla_w_uq[i], mla_w_ukv[i])
            naq, nak, nav, mq, mk, mv = _in_even(
                xs, g_row, mods[l], w, mla_q_norm[i][None], wuq, mla_kv_norm[i][None], wkv, cos_m, sin_m)
            a = _na_attention(naq, nak, nav, _na_bias_table(na_rpb[i]))
            b = _flash(mq, mk, mv, (), diff=False)
            w_out = e_w_out[i]
        else:
            lam_init = 0.8 - 0.6 * math.exp(-0.3 * l)
            w, w_gate = _odd_weights(o_w_in[i])
            dq, dk, dv, u = _in_odd(xs, g_row, mods[l], w, cos_d, sin_d)
            lamv = jnp.stack([diff_lq1[i], diff_lk1[i], diff_lq2[i], diff_lk2[i]])
            a = _flash(dq, dk, dv, (lamv, diff_subln[i][None]), diff=True, lam_init=lam_init)
            lru = lambda dr, hf: _lru_pass(
                u, lru_conv_w[i], lru_conv_b[i][None], lru_lambda[i, dr][None],
                _block_diag(lru_wa[i, dr]).astype(BF16), lru_ba[i, dr][None],
                _block_diag(lru_wx[i, dr]).astype(BF16), lru_bx[i, dr][None], hf, reverse=dr == 1)
            b = lru(1, lru(0, None))
            w_out = o_w_out[i]
        xs = _out_proj(a, b, g_row, w_gate, w_out.astype(BF16), xs, mods[l], final_g[None] if l == DEPTH - 1 else None)
    return xs[None]
```

```python
import functools
import math

import jax
import jax.numpy as jnp
import numpy as np
from jax import lax
from jax.experimental import pallas as pl
from jax.experimental.pallas import tpu as pltpu

F32 = jnp.float32
BF16 = jnp.bfloat16

D_MODEL = 1024
DEPTH = 4
GRID_W = 64
EPS = 1e-6
ROPE_BASE = 10000.0

NA_HEADS = 8
NA_HD = 64
NA_W = NA_HEADS * NA_HD
WIN_R = 8
WIN_C = 16

MLA_HEADS = 8
MLA_NOPE = 64
MLA_ROPE = 32
MLA_V = 64
MLA_QK = MLA_NOPE + MLA_ROPE
MLA_W = MLA_HEADS * MLA_V
Q_LORA = 384
KV_LORA = 256

DIFF_HEADS = 4
DIFF_HD = 64
DIFF_W = DIFF_HEADS * 2 * DIFF_HD

LRU_W = 512
LRU_BLOCKS = 8
LRU_BW = LRU_W // LRU_BLOCKS
CONV_W = 4
LRU_C = 8.0

LANES = 128
SUBLANES = 8
ROW_BLOCK = 256
KV_CHUNK = 1024
SOFTMAX_ROWS = 256
VT_ROWS = LANES + 16
VMEM_LIMIT = 56 * 1024 * 1024
NEG_BIG = -1e30
LOG2E = math.log2(math.e)

_NT = (((1,), (1,)), ((), ()))


def _dot_nt(a, b):
    return lax.dot_general(a, b, _NT, preferred_element_type=F32)


def _dot(a, b):
    return jnp.dot(a, b, preferred_element_type=F32)


def _params(sem):
    return pltpu.CompilerParams(dimension_semantics=sem, vmem_limit_bytes=VMEM_LIMIT)


def _mod_kernel(c_ref, w_ref, b_ref, o_ref):
    c = c_ref[...]
    s = c * jax.nn.sigmoid(c)
    o_ref[...] = _dot(s.astype(BF16), w_ref[...].astype(BF16)) + b_ref[...]


def _modulation(cvec, mod_w, mod_b):
    depth, d, d3 = mod_w.shape
    tn = 512
    return pl.pallas_call(
        _mod_kernel,
        grid=(depth, d3 // tn),
        in_specs=[
            pl.BlockSpec((SUBLANES, d), lambda l, j: (0, 0)),
            pl.BlockSpec((None, d, tn), lambda l, j: (l, 0, j)),
            pl.BlockSpec((None, 1, tn), lambda l, j: (l, 0, j)),
        ],
        out_specs=pl.BlockSpec((None, SUBLANES, tn), lambda l, j: (l, 0, j)),
        out_shape=jax.ShapeDtypeStruct((depth, SUBLANES, d3), F32),
        compiler_params=_params(("parallel", "parallel")),
        name="modulation",
    )(cvec, mod_w, mod_b.reshape(depth, 1, d3))


def _norm_mod(x, g, mod):
    d = x.shape[-1]
    y = x * lax.rsqrt(jnp.mean(x * x, axis=-1, keepdims=True) + EPS) * g
    return y * (1.0 + mod[:, d:2 * d]) + mod[:, 0:d]


def _rms(x, g):
    return x * lax.rsqrt(jnp.mean(x * x, axis=-1, keepdims=True) + EPS) * g


E_Q, E_K, E_V, E_LQ, E_LKV, E_KR, E_KRS, E_END = 0, 512, 1024, 1536, 1920, 2176, 2304, 2432


def _in_even_kernel(x_ref, g_ref, mod_ref, w_ref, qn_ref, wuq_ref, kvn_ref, wkv_ref, c_ref, s_ref,
                    naq_ref, nak_ref, nav_ref, mq_ref, mk_ref, mv_ref):
    hb = _norm_mod(x_ref[...], g_ref[...], mod_ref[...]).astype(BF16)
    npair = NA_HEADS // 2
    for ref, off in ((naq_ref, E_Q), (nak_ref, E_K), (nav_ref, E_V)):
        seg = _dot(hb, w_ref[:, off:off + NA_W])
        for p in range(npair):
            ref[p] = seg[:, p * LANES:(p + 1) * LANES].astype(BF16)
    cos = c_ref[...]
    sin = s_ref[...]
    qn = _rms(_dot(hb, w_ref[:, E_LQ:E_LKV]), qn_ref[...]).astype(BF16)
    width = MLA_HEADS * LANES
    q_main = _dot(qn, wuq_ref[:, 0:width])
    q_swap = _dot(qn, wuq_ref[:, width:2 * width])
    scale = MLA_QK ** -0.5 * LOG2E
    for h in range(MLA_HEADS):
        sl = slice(h * LANES, (h + 1) * LANES)
        mq_ref[h] = ((q_main[:, sl] * cos + q_swap[:, sl] * sin) * scale).astype(BF16)
    kvn = _rms(_dot(hb, w_ref[:, E_LKV:E_KR]), kvn_ref[...]).astype(BF16)
    kr = _dot(hb, w_ref[:, E_KR:E_KRS]) * cos + _dot(hb, w_ref[:, E_KRS:E_END]) * sin
    k_nope = _dot(kvn, wkv_ref[:, 0:width])
    for h in range(MLA_HEADS):
        mk_ref[h] = (k_nope[:, h * LANES:(h + 1) * LANES] + kr).astype(BF16)
    v = _dot(kvn, wkv_ref[:, width:width + MLA_W])
    for p in range(MLA_HEADS // 2):
        mv_ref[p, 0:LANES, :] = v[:, p * LANES:(p + 1) * LANES].T.astype(BF16)
    mv_ref[:, LANES:, :] = jnp.ones((MLA_HEADS // 2, VT_ROWS - LANES, ROW_BLOCK), BF16)


def _in_even(x, norm_g, mod, w, qn, wuq, kvn, wkv, cos, sin):
    t, d = x.shape
    nb = t // ROW_BLOCK
    full = lambda a: pl.BlockSpec(a.shape, lambda i: (0,) * a.ndim)
    rows = lambda c: pl.BlockSpec((ROW_BLOCK, c), lambda i: (i, 0))
    heads = lambda n: pl.BlockSpec((n, ROW_BLOCK, LANES), lambda i: (0, i, 0))
    hshape = lambda n: jax.ShapeDtypeStruct((n, t, LANES), BF16)
    return pl.pallas_call(
        _in_even_kernel,
        grid=(nb,),
        in_specs=[rows(d), full(norm_g),
                  pl.BlockSpec((None, 1, 3 * d), lambda i: (jnp.minimum(i, 1), 0, 0)),
                  full(w), full(qn), full(wuq), full(kvn), full(wkv), rows(LANES), rows(LANES)],
        out_specs=[heads(4), heads(4), heads(4), heads(8), heads(8),
                   pl.BlockSpec((4, VT_ROWS, ROW_BLOCK), lambda i: (0, 0, i))],
        out_shape=[hshape(4), hshape(4), hshape(4), hshape(8), hshape(8), jax.ShapeDtypeStruct((4, VT_ROWS, t), BF16)],
        compiler_params=_params(("parallel",)),
        name="in_proj_even",
    )(x, norm_g, mod, w, qn, wuq, kvn, wkv, cos, sin)


O_Q, O_QS, O_K, O_KS, O_V, O_U, O_END = 0, 512, 1024, 1536, 2048, 2560, 3072


def _in_odd_kernel(x_ref, g_ref, mod_ref, w_ref, c_ref, s_ref, dq_ref, dk_ref, dv_ref, u_ref):
    hb = _norm_mod(x_ref[...], g_ref[...], mod_ref[...]).astype(BF16)
    cos = c_ref[...]
    sin = s_ref[...]
    lane = lax.broadcasted_iota(jnp.int32, (1, LANES), 1)
    first = lane < DIFF_HD
    q = _dot(hb, w_ref[:, O_Q:O_QS])
    qs = _dot(hb, w_ref[:, O_QS:O_K])
    k = _dot(hb, w_ref[:, O_K:O_KS])
    ks = _dot(hb, w_ref[:, O_KS:O_V])
    v = _dot(hb, w_ref[:, O_V:O_U])
    for h in range(DIFF_HEADS):
        sl = slice(h * LANES, (h + 1) * LANES)
        qr = (q[:, sl] * cos + qs[:, sl] * sin) * LOG2E
        dq_ref[h, 0] = jnp.where(first, qr, 0.0).astype(BF16)
        dq_ref[h, 1] = jnp.where(first, 0.0, qr).astype(BF16)
        dk_ref[h] = (k[:, sl] * cos + ks[:, sl] * sin).astype(BF16)
        dv_ref[h, 0:LANES, :] = v[:, sl].T.astype(BF16)
    dv_ref[:, LANES:, :] = jnp.ones((DIFF_HEADS, VT_ROWS - LANES, ROW_BLOCK), BF16)
    u_ref[...] = _dot(hb, w_ref[:, O_U:O_END])


def _in_odd(x, norm_g, mod, w, cos, sin):
    t, d = x.shape
    nb = t // ROW_BLOCK
    full = lambda a: pl.BlockSpec(a.shape, lambda i: (0,) * a.ndim)
    rows = lambda c: pl.BlockSpec((ROW_BLOCK, c), lambda i: (i, 0))
    heads = pl.BlockSpec((DIFF_HEADS, ROW_BLOCK, LANES), lambda i: (0, i, 0))
    hshape = jax.ShapeDtypeStruct((DIFF_HEADS, t, LANES), BF16)
    return pl.pallas_call(
        _in_odd_kernel,
        grid=(nb,),
        in_specs=[rows(d), full(norm_g),
                  pl.BlockSpec((None, 1, 3 * d), lambda i: (jnp.minimum(i, 1), 0, 0)),
                  full(w), rows(LANES), rows(LANES)],
        out_specs=[pl.BlockSpec((DIFF_HEADS, 2, ROW_BLOCK, LANES), lambda i: (0, 0, i, 0)),
                   heads, pl.BlockSpec((DIFF_HEADS, VT_ROWS, ROW_BLOCK), lambda i: (0, 0, i)),
                   rows(LRU_W)],
        out_shape=[jax.ShapeDtypeStruct((DIFF_HEADS, 2, t, LANES), BF16), hshape,
                   jax.ShapeDtypeStruct((DIFF_HEADS, VT_ROWS, t), BF16),
                   jax.ShapeDtypeStruct((t, LRU_W), F32)],
        compiler_params=_params(("parallel",)),
        name="in_proj_odd",
    )(x, norm_g, mod, w, cos, sin)


def _sublane_all(op, x):
    for k in (4, 2, 1):
        x = op(x, pltpu.roll(x, k, 0))
    return x


def _flash_kernel(*refs, diff, lam_init):
    if diff:
        q_ref, qn_ref, k_ref, vt_ref, lamv_ref, subln_ref, o_ref = refs[:7]
    else:
        q_ref, qn_ref, k_ref, vt_ref, o_ref = refs[:5]
    m_sc, acc_sc, s0_sc, s1_sc, p0_sc, p1_sc, a0_sc, a1_sc, c0_sc, c1_sc = refs[-10:]
    i = pl.program_id(1)
    tq = q_ref.shape[1]
    dv = vt_ref.shape[0]
    t = vt_ref.shape[1]

    def scores(s_sc, c_sc, start, size, queries=q_ref):
        for s in range(2):
            kc = k_ref[pl.ds(start, size), :] if diff else k_ref[s, pl.ds(start, size), :]
            st = _dot_nt(kc, queries[s])
            s_sc[s, 0:size, :] = st
            c_sc[s] = jnp.max(st.reshape(size // SUBLANES, SUBLANES, tq), axis=0)

    def softmax(s_sc, c_sc, p_sc, a_sc, size):
        for s in range(2):
            m_prev = m_sc[s]
            m_new = jnp.maximum(m_prev, _sublane_all(jnp.maximum, c_sc[s]))
            alpha = jnp.exp2(m_prev - m_new)
            for r in range(0, size, SOFTMAX_ROWS):
                x = s_sc[s, r:r + SOFTMAX_ROWS, :].reshape(SOFTMAX_ROWS // SUBLANES, SUBLANES, tq) - m_new[None]
                p = jnp.exp2(x)
                p_sc[s, r:r + SOFTMAX_ROWS, :] = p.reshape(SOFTMAX_ROWS, tq).astype(BF16)
            a_sc[s] = alpha
            m_sc[s] = m_new

    def accumulate(p_sc, a_sc, start, size):
        vt = vt_ref[:, pl.ds(start, size)]
        for s in range(2):
            acc = acc_sc[s].reshape(dv // SUBLANES, SUBLANES, tq) * a_sc[s][None]
            acc_sc[s] = acc.reshape(dv, tq) + _dot(vt, p_sc[s, 0:size, :])

    bufs = ((s0_sc, c0_sc, p0_sc, a0_sc), (s1_sc, c1_sc, p1_sc, a1_sc))
    sizes = [KV_CHUNK] * ((t - ROW_BLOCK) // KV_CHUNK - 1) + [KV_CHUNK // 2] * 2
    starts = [ROW_BLOCK + sum(sizes[:c]) for c in range(len(sizes))]
    n_chunks = len(sizes)

    def result(s):
        l = acc_sc[s, LANES:LANES + SUBLANES, :]
        o_t = acc_sc[s, 0:LANES, :].reshape(LANES // SUBLANES, SUBLANES, tq) / l[None]
        return o_t.reshape(LANES, tq).T

    def finish():
        o0 = result(0)
        o1 = result(1)
        if diff:
            lv = lamv_ref[...]
            lam = (jnp.exp(jnp.sum(lv[0:1] * lv[1:2], axis=-1, keepdims=True))
                   - jnp.exp(jnp.sum(lv[2:3] * lv[3:4], axis=-1, keepdims=True)) + lam_init)
            o = o0 - lam * o1
            o_ref[...] = _rms(o, subln_ref[...]) * (1.0 - lam_init)
        else:
            lane = lax.broadcasted_iota(jnp.int32, (1, LANES), 1)
            o_ref[...] = jnp.where(lane < MLA_V, o0, o1)

    m_sc[...] = jnp.full(m_sc.shape, NEG_BIG, F32)
    acc_sc[...] = jnp.zeros(acc_sc.shape, F32)

    @pl.when(i == 0)
    def _():
        scores(s0_sc, c0_sc, 0, ROW_BLOCK)
        softmax(s0_sc, c0_sc, p0_sc, a0_sc, ROW_BLOCK)
        scores(s1_sc, c1_sc, starts[0], sizes[0], qn_ref)
        accumulate(p0_sc, a0_sc, 0, ROW_BLOCK)
        finish()

    @pl.when(i > 0)
    def _():
        for n in range(1, n_chunks + 1):
            s_a, c_a, p_a, a_a = bufs[(n - 1) % 2]
            s_b, c_b, p_b, a_b = bufs[n % 2]
            if n < n_chunks:
                scores(s_a, c_a, starts[n], sizes[n])
            else:
                scores(s_a, c_a, 0, ROW_BLOCK)
            if n > 1:
                accumulate(p_a, a_a, starts[n - 2], sizes[n - 2])
            softmax(s_b, c_b, p_b, a_b, sizes[n - 1])
        s_a, c_a, p_a, a_a = bufs[n_chunks % 2]
        s_b, c_b, p_b, a_b = bufs[(n_chunks + 1) % 2]
        accumulate(p_a, a_a, starts[-1], sizes[-1])
        softmax(s_b, c_b, p_b, a_b, ROW_BLOCK)
        scores(s1_sc, c1_sc, starts[0], sizes[0], qn_ref)
        accumulate(p_b, a_b, 0, ROW_BLOCK)
        finish()


def _flash(q, k, vt, extra, *, diff, lam_init=0.0):
    groups, t = vt.shape[0], vt.shape[2]
    nb = t // ROW_BLOCK
    assert (t - ROW_BLOCK) % (2 * KV_CHUNK) == 0
    if diff:
        q_spec = pl.BlockSpec((None, 2, ROW_BLOCK, LANES), lambda g, i: (g, 0, i, 0))
        qn_spec = pl.BlockSpec((None, 2, ROW_BLOCK, LANES), lambda g, i: (g, 0, jnp.minimum(i + 1, nb - 1), 0))
        k_spec = pl.BlockSpec((None, t, LANES), lambda g, i: (g, 0, 0))
        extra_specs = [pl.BlockSpec(e.shape, lambda g, i: (0, 0)) for e in extra]
    else:
        q_spec = pl.BlockSpec((2, ROW_BLOCK, LANES), lambda g, i: (g, i, 0))
        qn_spec = pl.BlockSpec((2, ROW_BLOCK, LANES), lambda g, i: (g, jnp.minimum(i + 1, nb - 1), 0))
        k_spec = pl.BlockSpec((2, t, LANES), lambda g, i: (g, 0, 0))
        extra_specs = []
    return pl.pallas_call(
        functools.partial(_flash_kernel, diff=diff, lam_init=lam_init),
        grid=(groups, nb),
        in_specs=[q_spec, qn_spec, k_spec, pl.BlockSpec((None, VT_ROWS, t), lambda g, i: (g, 0, 0))] + extra_specs,
        out_specs=pl.BlockSpec((ROW_BLOCK, LANES), lambda g, i: (i, g)),
        out_shape=jax.ShapeDtypeStruct((t, groups * LANES), F32),
        scratch_shapes=[pltpu.VMEM((2, SUBLANES, ROW_BLOCK), F32), pltpu.VMEM((2, VT_ROWS, ROW_BLOCK), F32)]
        + [pltpu.VMEM((2, KV_CHUNK, ROW_BLOCK), F32)] * 2 + [pltpu.VMEM((2, KV_CHUNK, ROW_BLOCK), BF16)] * 2
        + [pltpu.VMEM((2, SUBLANES, ROW_BLOCK), F32)] * 4,
        compiler_params=_params(("parallel", "arbitrary")),
        name="flash_diff" if diff else "flash_mla",
    )(q, q, k, vt, *extra)


NA_ROWS_PER_STEP = ROW_BLOCK // GRID_W


def _na_kernel(q_ref, k_ref, v_ref, bt_ref, o_ref, *, grid_rows):
    i = pl.program_id(1)
    lane = lax.broadcasted_iota(jnp.int32, (1, LANES), 1)
    first = lane < NA_HD
    kctx = k_ref[0:ROW_BLOCK, :]
    vctx = v_ref[0:ROW_BLOCK, :]

    def stacked(q):
        zero = jnp.zeros_like(q)
        return jnp.concatenate([jnp.where(first, q, zero), jnp.where(first, zero, q)], axis=0)

    @pl.when(i == 0)
    def _():
        q2 = stacked(q_ref[...])
        sc = _dot_nt(q2, kctx)
        p = jnp.exp(sc - jnp.max(sc, axis=-1, keepdims=True))
        o = _dot(p.astype(BF16), vctx) / jnp.sum(p, axis=-1, keepdims=True)
        o_ref[...] = jnp.where(first, o[0:ROW_BLOCK], o[ROW_BLOCK:])

    @pl.when(i > 0)
    def _():
        nkeys = WIN_R * GRID_W
        starts, s_lat, s_ctx = [], [], []
        for rr in range(NA_ROWS_PER_STEP):
            r = (i - 1) * NA_ROWS_PER_STEP + rr
            r0 = jnp.clip(r - WIN_R // 2, 0, grid_rows - WIN_R)
            start = pl.multiple_of(ROW_BLOCK + r0 * GRID_W, GRID_W)
            q2 = stacked(q_ref[rr * GRID_W:(rr + 1) * GRID_W, :])
            starts.append(start)
            s_lat.append(_dot_nt(q2, k_ref[pl.ds(start, nkeys), :]) + bt_ref[r0 - r + WIN_R - 1])
            s_ctx.append(_dot_nt(q2, kctx))
        p_lat, p_ctx, denom = [], [], []
        for rr in range(NA_ROWS_PER_STEP):
            m = jnp.maximum(jnp.max(s_lat[rr], axis=-1, keepdims=True), jnp.max(s_ctx[rr], axis=-1, keepdims=True))
            pl_, pc_ = jnp.exp(s_lat[rr] - m), jnp.exp(s_ctx[rr] - m)
            denom.append(jnp.sum(pl_, axis=-1, keepdims=True) + jnp.sum(pc_, axis=-1, keepdims=True))
            p_lat.append(pl_.astype(BF16))
            p_ctx.append(pc_.astype(BF16))
        for rr in range(NA_ROWS_PER_STEP):
            o = (_dot(p_lat[rr], v_ref[pl.ds(starts[rr], nkeys), :]) + _dot(p_ctx[rr], vctx)) / denom[rr]
            o_ref[rr * GRID_W:(rr + 1) * GRID_W, :] = jnp.where(first, o[0:GRID_W], o[GRID_W:])


def _na_attention(q, k, v, bias):
    npair, t = q.shape[0], q.shape[1]
    nb = t // ROW_BLOCK
    grid_rows = (t - ROW_BLOCK) // GRID_W
    resident = pl.BlockSpec((None, t, LANES), lambda p, i: (p, 0, 0))
    return pl.pallas_call(
        functools.partial(_na_kernel, grid_rows=grid_rows),
        grid=(npair, nb),
        in_specs=[pl.BlockSpec((None, ROW_BLOCK, LANES), lambda p, i: (p, i, 0)), resident, resident,
                  pl.BlockSpec((None,) + bias.shape[1:], lambda p, i: (p, 0, 0, 0))],
        out_specs=pl.BlockSpec((ROW_BLOCK, LANES), lambda p, i: (i, p)),
        out_shape=jax.ShapeDtypeStruct((t, npair * LANES), F32),
        compiler_params=_params(("parallel", "arbitrary")),
        name="na_attention",
    )(q, k, v, bias)


def _na_bias_table(rpb):
    w = np.arange(GRID_W)
    c0 = np.clip(w - WIN_C // 2, 0, GRID_W - WIN_C)
    inside = (w[None, :] >= c0[:, None]) & (w[None, :] < c0[:, None] + WIN_C)
    edge = GRID_W - WIN_C
    pad = jnp.pad(rpb, ((0, 0), (0, 0), (edge, edge)))
    col = jnp.stack([pad[:, :, GRID_W - 1 - wq:2 * GRID_W - 1 - wq] for wq in range(GRID_W)], axis=2)
    col = jnp.where(inside[None, None], col, NEG_BIG)
    tab = jnp.stack([col[:, var:var + WIN_R] for var in range(WIN_R)], axis=1)
    tab = tab.transpose(0, 1, 3, 2, 4).reshape(NA_HEADS // 2, 2, WIN_R, GRID_W, WIN_R * GRID_W)
    return tab.transpose(0, 2, 1, 3, 4).reshape(NA_HEADS // 2, WIN_R, 2 * GRID_W, WIN_R * GRID_W)


def _out_kernel(*refs, final):
    if final:
        a_ref, b_ref, g_ref, wg_ref, w_ref, x_ref, mod_ref, fg_ref, o_ref = refs
    else:
        a_ref, b_ref, g_ref, wg_ref, w_ref, x_ref, mod_ref, o_ref = refs
    half = a_ref.shape[-1]
    d = x_ref.shape[-1]
    hb = _norm_mod(x_ref[...], g_ref[...], mod_ref[...]).astype(BF16)
    ga = _dot(hb, wg_ref[:, 0:half])
    gb = _dot(hb, wg_ref[:, half:2 * half])
    ya = (a_ref[...] * (ga * jax.nn.sigmoid(ga))).astype(BF16)
    yb = (b_ref[...] * (gb * jax.nn.sigmoid(gb))).astype(BF16)
    y = _dot(ya, w_ref[0:half, :]) + _dot(yb, w_ref[half:2 * half, :])
    x_new = x_ref[...] + mod_ref[:, 2 * d:3 * d] * y
    o_ref[...] = _rms(x_new, fg_ref[...]) if final else x_new


def _out_proj(a, b, norm_g, w_gate, w, x, mod, final_g=None):
    t, d = x.shape
    nb = t // ROW_BLOCK
    final = final_g is not None
    rows = lambda c: pl.BlockSpec((ROW_BLOCK, c), lambda i: (i, 0))
    in_specs = [rows(a.shape[1]), rows(b.shape[1]), pl.BlockSpec(norm_g.shape, lambda i: (0, 0)),
                pl.BlockSpec(w_gate.shape, lambda i: (0, 0)), pl.BlockSpec(w.shape, lambda i: (0, 0)), rows(d),
                pl.BlockSpec((None, 1, 3 * d), lambda i: (jnp.minimum(i, 1), 0, 0))]
    args = [a, b, norm_g, w_gate, w, x, mod]
    if final:
        in_specs.append(pl.BlockSpec((1, d), lambda i: (0, 0)))
        args.append(final_g)
    return pl.pallas_call(
        functools.partial(_out_kernel, final=final),
        grid=(nb,),
        in_specs=in_specs,
        out_specs=pl.BlockSpec((ROW_BLOCK, d), lambda i: (jnp.maximum(i - 1, 0), 0)) if final else rows(d),
        out_shape=jax.ShapeDtypeStruct((t - ROW_BLOCK if final else t, d), F32),
        compiler_params=_params(("arbitrary",) if final else ("parallel",)),
        name="out_proj_final" if final else "out_proj",
    )(*args)


def _lru_kernel(*refs, reverse, nb):
    if reverse:
        (x_ref, xp_ref, xn_ref, cw_ref, cb_ref, lam_ref, wa_ref, ba_ref, wx_ref, bx_ref, hf_ref,
         o_ref, xs_sc, a_sc, b_sc, carry_sc) = refs
    else:
        (x_ref, xp_ref, xn_ref, cw_ref, cb_ref, lam_ref, wa_ref, ba_ref, wx_ref, bx_ref,
         o_ref, xs_sc, a_sc, b_sc, carry_sc) = refs
    step = pl.program_id(0)
    blk = jnp.where(step == 0, 0, nb - step) if reverse else step
    tb = x_ref.shape[0]

    @pl.when(step == 0)
    def _():
        carry_sc[...] = jnp.zeros_like(carry_sc)

    prev_ok = blk >= 2
    next_ok = (blk >= 1) & (blk < nb - 1)
    xs_sc[0:SUBLANES, :] = jnp.where(prev_ok, xp_ref[...], 0.0)
    xs_sc[SUBLANES:SUBLANES + tb, :] = x_ref[...]
    xs_sc[SUBLANES + tb:2 * SUBLANES + tb, :] = jnp.where(next_ok, xn_ref[...], 0.0)
    u = cb_ref[...]
    for j in range(CONV_W):
        off = SUBLANES + j - CONV_W // 2
        u = u + xs_sc[off:off + tb, :] * cw_ref[j:j + 1, :]

    ub = u.astype(BF16)
    r = jax.nn.sigmoid(_dot(ub, wa_ref[...]) + ba_ref[...])
    gi = jax.nn.sigmoid(_dot(ub, wx_ref[...]) + bx_ref[...])
    nl = -lam_ref[...]
    softplus = jnp.maximum(nl, 0.0) + jnp.log1p(jnp.exp(-jnp.abs(nl)))
    log_a = -LRU_C * r * softplus
    a_sc[...] = jnp.exp(log_a)
    b_sc[...] = jnp.sqrt(1.0 - jnp.exp(2.0 * log_a)) * (gi * u)

    row = lax.broadcasted_iota(jnp.int32, (SUBLANES, 1), 0)
    nchunk = tb // SUBLANES

    def chunk(c, carry):
        cc = nchunk - 1 - c if reverse else c
        start = pl.multiple_of(cc * SUBLANES, SUBLANES)
        a = a_sc[pl.ds(start, SUBLANES), :]
        b = b_sc[pl.ds(start, SUBLANES), :]
        for k in (1, 2, 4):
            shift = SUBLANES - k if reverse else k
            valid = (row < SUBLANES - k) if reverse else (row >= k)
            a_sh = pltpu.roll(a, shift, 0)
            b_sh = pltpu.roll(b, shift, 0)
            b = jnp.where(valid, a * b_sh + b, b)
            a = jnp.where(valid, a * a_sh, a)
        h = b + a * carry
        if reverse:
            o_ref[pl.ds(start, SUBLANES), :] = h + hf_ref[pl.ds(start, SUBLANES), :]
            new = h[0:1, :]
        else:
            o_ref[pl.ds(start, SUBLANES), :] = h
            new = h[SUBLANES - 1:SUBLANES, :]
        return jnp.broadcast_to(new, carry.shape)

    carry_sc[...] = lax.fori_loop(0, nchunk, chunk, carry_sc[...], unroll=True)


def _lru_pass(x, conv_w, conv_b, lam, wa, ba, wx, bx, hf, *, reverse):
    t, c = x.shape
    tb = ROW_BLOCK
    nb = t // tb
    per8 = tb // SUBLANES
    nb8 = t // SUBLANES
    if reverse:
        blk = lambda s: jnp.where(s == 0, 0, nb - s)
    else:
        blk = lambda s: s
    full = lambda a: pl.BlockSpec(a.shape, lambda s: (0,) * a.ndim)
    main = pl.BlockSpec((tb, c), lambda s: (blk(s), 0))
    args = [x, x, x, conv_w, conv_b, lam, wa, ba, wx, bx]
    in_specs = [main,
                pl.BlockSpec((SUBLANES, c), lambda s: (jnp.maximum(blk(s) * per8 - 1, 0), 0)),
                pl.BlockSpec((SUBLANES, c), lambda s: (jnp.minimum((blk(s) + 1) * per8, nb8 - 1), 0)),
                full(conv_w), full(conv_b), full(lam), full(wa), full(ba), full(wx), full(bx)]
    if reverse:
        args.append(hf)
        in_specs.append(main)
    return pl.pallas_call(
        functools.partial(_lru_kernel, reverse=reverse, nb=nb),
        grid=(nb,),
        in_specs=in_specs,
        out_specs=main,
        out_shape=jax.ShapeDtypeStruct((t, c), F32),
        scratch_shapes=[pltpu.VMEM((tb + 2 * SUBLANES, c), F32), pltpu.VMEM((tb, c), F32),
                        pltpu.VMEM((tb, c), F32), pltpu.VMEM((SUBLANES, c), F32)],
        compiler_params=_params(("arbitrary",)),
        name="lru_bwd" if reverse else "lru_fwd",
    )(*args)


def _block_diag(w):
    g, bw, _ = w.shape
    eye = jnp.eye(g, dtype=w.dtype)
    return (eye[:, None, :, None] * w[:, :, None, :]).reshape(g * bw, g * bw)


def _rope_tables(n, ctx_len, rot_dim, lead, tail, repeat):
    t = jnp.arange(n)
    row = (t // GRID_W).astype(F32)
    col = (t % GRID_W).astype(F32)
    n_freq = rot_dim // 4
    inv = ROPE_BASE ** (-jnp.arange(n_freq, dtype=F32) / n_freq)
    ang = jnp.concatenate([row[:, None] * inv, col[:, None] * inv], axis=-1)
    cos, sin = jnp.cos(ang), jnp.sin(ang)
    cos_l = jnp.concatenate([jnp.ones((n, lead), F32)] + [cos, cos] * repeat + [jnp.ones((n, tail), F32)], axis=-1)
    sin_l = jnp.concatenate([jnp.zeros((n, lead), F32)] + [-sin, sin] * repeat + [jnp.zeros((n, tail), F32)], axis=-1)
    cos_t = jnp.concatenate([jnp.ones((ctx_len, LANES), F32), cos_l], axis=0)
    sin_t = jnp.concatenate([jnp.zeros((ctx_len, LANES), F32), sin_l], axis=0)
    return cos_t, sin_t


def _even_weights(w_in, w_uq, w_ukv):
    d = w_in.shape[0]
    q, k, v, g_na, lq, lkv, kr, g_mla = jnp.split(
        w_in, [512, 1024, 1536, 2048, 2048 + Q_LORA, 2048 + Q_LORA + KV_LORA, 2048 + Q_LORA + KV_LORA + MLA_ROPE], axis=1)
    half = MLA_ROPE // 2
    pad = LANES - MLA_QK
    kr_pad = jnp.concatenate([jnp.zeros((d, MLA_NOPE), F32), kr, jnp.zeros((d, pad), F32)], axis=1)
    krs_pad = jnp.concatenate([jnp.zeros((d, MLA_NOPE), F32), kr[:, half:], kr[:, :half], jnp.zeros((d, pad), F32)], axis=1)
    w = jnp.concatenate([q * (NA_HD ** -0.5), k, v, lq, lkv, kr_pad, krs_pad], axis=1).astype(BF16)
    w_gate = jnp.concatenate([g_na, g_mla], axis=1).astype(BF16)
    uq = w_uq.reshape(Q_LORA, MLA_HEADS, MLA_QK)
    zq = jnp.zeros((Q_LORA, MLA_HEADS, pad), F32)
    uq_main = jnp.concatenate([uq, zq], axis=-1)
    uq_swap = jnp.concatenate([jnp.zeros((Q_LORA, MLA_HEADS, MLA_NOPE), F32), uq[..., MLA_NOPE + half:],
                               uq[..., MLA_NOPE:MLA_NOPE + half], zq], axis=-1)
    wuq = jnp.concatenate([uq_main.reshape(Q_LORA, -1), uq_swap.reshape(Q_LORA, -1)], axis=1).astype(BF16)
    ukv = w_ukv.reshape(KV_LORA, MLA_HEADS, MLA_NOPE + MLA_V)
    k_nope = jnp.concatenate([ukv[..., :MLA_NOPE], jnp.zeros((KV_LORA, MLA_HEADS, LANES - MLA_NOPE), F32)], axis=-1)
    wkv = jnp.concatenate([k_nope.reshape(KV_LORA, -1), ukv[..., MLA_NOPE:].reshape(KV_LORA, -1)], axis=1).astype(BF16)
    return w, w_gate, wuq, wkv


def _odd_weights(w_in):
    d = w_in.shape[0]
    q, k, v, g_d, u, g_lru = jnp.split(w_in, [512, 1024, 1536, 2048, 2560], axis=1)

    def swap(m):
        return m.reshape(d, DIFF_HEADS, 2, 2, DIFF_HD // 2)[:, :, :, ::-1, :].reshape(d, DIFF_W)

    scale = DIFF_HD ** -0.5
    w = jnp.concatenate([q * scale, swap(q) * scale, k, swap(k), v, u], axis=1).astype(BF16)
    return w, jnp.concatenate([g_d, g_lru], axis=1).astype(BF16)


def kernel(x, c, ctx, c_ctx, mod_w, mod_b, norm_g, final_g, e_w_in, e_w_out, na_rpb, mla_q_norm, mla_w_uq, mla_kv_norm, mla_w_ukv, o_w_in, o_w_out, diff_lq1, diff_lk1, diff_lq2, diff_lk2, diff_subln, lru_conv_w, lru_conv_b, lru_lambda, lru_wa, lru_ba, lru_wx, lru_bx):
    batch, n, d = x.shape
    ctx_len = ctx.shape[1]
    assert batch == 1 and ctx_len == ROW_BLOCK and d == D_MODEL
    assert n // GRID_W >= WIN_R

    cvec = jnp.concatenate([c_ctx[None], c, jnp.zeros((SUBLANES - 2, d), F32)], axis=0)
    mods = _modulation(cvec, mod_w, mod_b)[:, 0:2].reshape(DEPTH, 2, 1, 3 * d)

    cos_m, sin_m = _rope_tables(n, ctx_len, MLA_ROPE, MLA_NOPE, LANES - MLA_QK, 1)
    cos_d, sin_d = _rope_tables(n, ctx_len, DIFF_HD, 0, 0, 2)

    xs = jnp.concatenate([ctx[0], x[0]], axis=0)
    for l in range(DEPTH):
        i = l // 2
        g_row = norm_g[l][None]
        if l % 2 == 0:
            w, w_gate, wuq, wkv = _even_weights(e_w_in[i], mla_w_uq[i], mla_w_ukv[i])
            naq, nak, nav, mq, mk, mv = _in_even(
                xs, g_row, mods[l], w, mla_q_norm[i][None], wuq, mla_kv_norm[i][None], wkv, cos_m, sin_m)
            a = _na_attention(naq, nak, nav, _na_bias_table(na_rpb[i]))
            b = _flash(mq, mk, mv, (), diff=False)
            w_out = e_w_out[i]
        else:
            lam_init = 0.8 - 0.6 * math.exp(-0.3 * l)
            w, w_gate = _odd_weights(o_w_in[i])
            dq, dk, dv, u = _in_odd(xs, g_row, mods[l], w, cos_d, sin_d)
            lamv = jnp.stack([diff_lq1[i], diff_lk1[i], diff_lq2[i], diff_lk2[i]])
            a = _flash(dq, dk, dv, (lamv, diff_subln[i][None]), diff=True, lam_init=lam_init)
            lru = lambda dr, hf: _lru_pass(
                u, lru_conv_w[i], lru_conv_b[i][None], lru_lambda[i, dr][None],
                _block_diag(lru_wa[i, dr]).astype(BF16), lru_ba[i, dr][None],
                _block_diag(lru_wx[i, dr]).astype(BF16), lru_bx[i, dr][None], hf, reverse=dr == 1)
            b = lru(1, lru(0, None))
            w_out = o_w_out[i]
        xs = _out_proj(a, b, g_row, w_gate, w_out.astype(BF16), xs, mods[l], final_g[None] if l == DEPTH - 1 else None)
    return xs[None]
```

```python
import functools
import math

import jax
import jax.numpy as jnp
import numpy as np
from jax import lax
from jax.experimental import pallas as pl
from jax.experimental.pallas import tpu as pltpu

F32 = jnp.float32
BF16 = jnp.bfloat16

D_MODEL = 1024
DEPTH = 4
GRID_W = 64
EPS = 1e-6
ROPE_BASE = 10000.0

NA_HEADS = 8
NA_HD = 64
NA_W = NA_HEADS * NA_HD
WIN_R = 8
WIN_C = 16

MLA_HEADS = 8
MLA_NOPE = 64
MLA_ROPE = 32
MLA_V = 64
MLA_QK = MLA_NOPE + MLA_ROPE
MLA_W = MLA_HEADS * MLA_V
Q_LORA = 384
KV_LORA = 256

DIFF_HEADS = 4
DIFF_HD = 64
DIFF_W = DIFF_HEADS * 2 * DIFF_HD

LRU_W = 512
LRU_BLOCKS = 8
LRU_BW = LRU_W // LRU_BLOCKS
CONV_W = 4
LRU_C = 8.0

LANES = 128
SUBLANES = 8
ROW_BLOCK = 256
KV_CHUNK = 1024
SOFTMAX_ROWS = 256
VT_ROWS = LANES + 16
VMEM_LIMIT = 56 * 1024 * 1024
NEG_BIG = -1e30
LOG2E = math.log2(math.e)

_NT = (((1,), (1,)), ((), ()))


def _dot_nt(a, b):
    return lax.dot_general(a, b, _NT, preferred_element_type=F32)


def _dot(a, b):
    return jnp.dot(a, b, preferred_element_type=F32)


def _params(sem):
    return pltpu.CompilerParams(dimension_semantics=sem, vmem_limit_bytes=VMEM_LIMIT)


def _mod_kernel(c_ref, w_ref, b_ref, o_ref):
    c = c_ref[...]
    s = c * jax.nn.sigmoid(c)
    o_ref[...] = _dot(s.astype(BF16), w_ref[...].astype(BF16)) + b_ref[...]


def _modulation(cvec, mod_w, mod_b):
    depth, d, d3 = mod_w.shape
    tn = 512
    return pl.pallas_call(
        _mod_kernel,
        grid=(depth, d3 // tn),
        in_specs=[
            pl.BlockSpec((SUBLANES, d), lambda l, j: (0, 0)),
            pl.BlockSpec((None, d, tn), lambda l, j: (l, 0, j)),
            pl.BlockSpec((None, 1, tn), lambda l, j: (l, 0, j)),
        ],
        out_specs=pl.BlockSpec((None, SUBLANES, tn), lambda l, j: (l, 0, j)),
        out_shape=jax.ShapeDtypeStruct((depth, SUBLANES, d3), F32),
        compiler_params=_params(("parallel", "parallel")),
        name="modulation",
    )(cvec, mod_w, mod_b.reshape(depth, 1, d3))


def _norm_mod(x, g, mod):
    d = x.shape[-1]
    y = x * lax.rsqrt(jnp.mean(x * x, axis=-1, keepdims=True) + EPS) * g
    return y * (1.0 + mod[:, d:2 * d]) + mod[:, 0:d]


def _rms(x, g):
    return x * lax.rsqrt(jnp.mean(x * x, axis=-1, keepdims=True) + EPS) * g


E_Q, E_K, E_V, E_LQ, E_LKV, E_KR, E_END = 0, 512, 1024, 1536, 1920, 2176, 2304


def _in_even_kernel(x_ref, g_ref, mod_ref, w_ref, qn_ref, wuq_ref, kvn_ref, wkv_ref, c_ref, s_ref,
                    naq_ref, nak_ref, nav_ref, mq_ref, mk_ref, mv_ref):
    hb = _norm_mod(x_ref[...], g_ref[...], mod_ref[...]).astype(BF16)
    npair = NA_HEADS // 2
    for ref, off in ((naq_ref, E_Q), (nak_ref, E_K), (nav_ref, E_V)):
        seg = _dot(hb, w_ref[:, off:off + NA_W])
        for p in range(npair):
            ref[p] = seg[:, p * LANES:(p + 1) * LANES].astype(BF16)
    cos = c_ref[...]
    sin = s_ref[...]
    lane = lax.broadcasted_iota(jnp.int32, (1, LANES), 1)
    low = lane < MLA_NOPE + MLA_ROPE // 2

    def swap_halves(x):
        return jnp.where(low, pltpu.roll(x, LANES - MLA_ROPE // 2, 1), pltpu.roll(x, MLA_ROPE // 2, 1))

    qn = _rms(_dot(hb, w_ref[:, E_LQ:E_LKV]), qn_ref[...]).astype(BF16)
    width = MLA_HEADS * LANES
    q_main = _dot(qn, wuq_ref[...])
    scale = MLA_QK ** -0.5 * LOG2E
    for h in range(MLA_HEADS):
        qh = q_main[:, h * LANES:(h + 1) * LANES]
        mq_ref[h] = ((qh * cos + swap_halves(qh) * sin) * scale).astype(BF16)
    kvn = _rms(_dot(hb, w_ref[:, E_LKV:E_KR]), kvn_ref[...]).astype(BF16)
    kr = _dot(hb, w_ref[:, E_KR:E_END])
    kr = kr * cos + swap_halves(kr) * sin
    k_nope = _dot(kvn, wkv_ref[:, 0:width])
    for h in range(MLA_HEADS):
        mk_ref[h] = (k_nope[:, h * LANES:(h + 1) * LANES] + kr).astype(BF16)
    v = _dot(kvn, wkv_ref[:, width:width + MLA_W])
    for p in range(MLA_HEADS // 2):
        mv_ref[p, 0:LANES, :] = v[:, p * LANES:(p + 1) * LANES].T.astype(BF16)
    mv_ref[:, LANES:, :] = jnp.ones((MLA_HEADS // 2, VT_ROWS - LANES, ROW_BLOCK), BF16)


def _in_even(x, norm_g, mod, w, qn, wuq, kvn, wkv, cos, sin):
    t, d = x.shape
    nb = t // ROW_BLOCK
    full = lambda a: pl.BlockSpec(a.shape, lambda i: (0,) * a.ndim)
    rows = lambda c: pl.BlockSpec((ROW_BLOCK, c), lambda i: (i, 0))
    heads = lambda n: pl.BlockSpec((n, ROW_BLOCK, LANES), lambda i: (0, i, 0))
    hshape = lambda n: jax.ShapeDtypeStruct((n, t, LANES), BF16)
    return pl.pallas_call(
        _in_even_kernel,
        grid=(nb,),
        in_specs=[rows(d), full(norm_g),
                  pl.BlockSpec((None, 1, 3 * d), lambda i: (jnp.minimum(i, 1), 0, 0)),
                  full(w), full(qn), full(wuq), full(kvn), full(wkv), rows(LANES), rows(LANES)],
        out_specs=[heads(4), heads(4), heads(4), heads(8), heads(8),
                   pl.BlockSpec((4, VT_ROWS, ROW_BLOCK), lambda i: (0, 0, i))],
        out_shape=[hshape(4), hshape(4), hshape(4), hshape(8), hshape(8), jax.ShapeDtypeStruct((4, VT_ROWS, t), BF16)],
        compiler_params=_params(("parallel",)),
        name="in_proj_even",
    )(x, norm_g, mod, w, qn, wuq, kvn, wkv, cos, sin)


O_Q, O_K, O_V, O_U, O_END = 0, 512, 1024, 1536, 2048


def _in_odd_kernel(x_ref, g_ref, mod_ref, w_ref, c_ref, s_ref, dq_ref, dk_ref, dv_ref, u_ref):
    hb = _norm_mod(x_ref[...], g_ref[...], mod_ref[...]).astype(BF16)
    cos = c_ref[...]
    sin = s_ref[...]
    lane = lax.broadcasted_iota(jnp.int32, (1, LANES), 1)
    first = lane < DIFF_HD
    low = lane % DIFF_HD < DIFF_HD // 2

    def swap_halves(x):
        return jnp.where(low, pltpu.roll(x, LANES - DIFF_HD // 2, 1), pltpu.roll(x, DIFF_HD // 2, 1))

    q = _dot(hb, w_ref[:, O_Q:O_K])
    k = _dot(hb, w_ref[:, O_K:O_V])
    v = _dot(hb, w_ref[:, O_V:O_U])
    for h in range(DIFF_HEADS):
        sl = slice(h * LANES, (h + 1) * LANES)
        qr = (q[:, sl] * cos + swap_halves(q[:, sl]) * sin) * LOG2E
        dq_ref[h, 0] = jnp.where(first, qr, 0.0).astype(BF16)
        dq_ref[h, 1] = jnp.where(first, 0.0, qr).astype(BF16)
        dk_ref[h] = (k[:, sl] * cos + swap_halves(k[:, sl]) * sin).astype(BF16)
        dv_ref[h, 0:LANES, :] = v[:, sl].T.astype(BF16)
    dv_ref[:, LANES:, :] = jnp.ones((DIFF_HEADS, VT_ROWS - LANES, ROW_BLOCK), BF16)
    u_ref[...] = _dot(hb, w_ref[:, O_U:O_END])


def _in_odd(x, norm_g, mod, w, cos, sin):
    t, d = x.shape
    nb = t // ROW_BLOCK
    full = lambda a: pl.BlockSpec(a.shape, lambda i: (0,) * a.ndim)
    rows = lambda c: pl.BlockSpec((ROW_BLOCK, c), lambda i: (i, 0))
    heads = pl.BlockSpec((DIFF_HEADS, ROW_BLOCK, LANES), lambda i: (0, i, 0))
    hshape = jax.ShapeDtypeStruct((DIFF_HEADS, t, LANES), BF16)
    return pl.pallas_call(
        _in_odd_kernel,
        grid=(nb,),
        in_specs=[rows(d), full(norm_g),
                  pl.BlockSpec((None, 1, 3 * d), lambda i: (jnp.minimum(i, 1), 0, 0)),
                  full(w), rows(LANES), rows(LANES)],
        out_specs=[pl.BlockSpec((DIFF_HEADS, 2, ROW_BLOCK, LANES), lambda i: (0, 0, i, 0)),
                   heads, pl.BlockSpec((DIFF_HEADS, VT_ROWS, ROW_BLOCK), lambda i: (0, 0, i)),
                   rows(LRU_W)],
        out_shape=[jax.ShapeDtypeStruct((DIFF_HEADS, 2, t, LANES), BF16), hshape,
                   jax.ShapeDtypeStruct((DIFF_HEADS, VT_ROWS, t), BF16),
                   jax.ShapeDtypeStruct((t, LRU_W), F32)],
        compiler_params=_params(("parallel",)),
        name="in_proj_odd",
    )(x, norm_g, mod, w, cos, sin)


def _sublane_all(op, x):
    for k in (4, 2, 1):
        x = op(x, pltpu.roll(x, k, 0))
    return x


def _flash_kernel(*refs, diff, lam_init):
    if diff:
        q_ref, qn_ref, k_ref, vt_ref, lamv_ref, subln_ref, o_ref = refs[:7]
    else:
        q_ref, qn_ref, k_ref, vt_ref, o_ref = refs[:5]
    m_sc, acc_sc, s0_sc, s1_sc, p0_sc, p1_sc, a0_sc, a1_sc, c0_sc, c1_sc = refs[-10:]
    i = pl.program_id(1)
    tq = q_ref.shape[1]
    dv = vt_ref.shape[0]
    t = vt_ref.shape[1]

    def scores(s_sc, c_sc, start, size, queries=q_ref):
        for s in range(2):
            kc = k_ref[pl.ds(start, size), :] if diff else k_ref[s, pl.ds(start, size), :]
            st = _dot_nt(kc, queries[s])
            s_sc[s, 0:size, :] = st
            c_sc[s] = jnp.max(st.reshape(size // SUBLANES, SUBLANES, tq), axis=0)

    def softmax(s_sc, c_sc, p_sc, a_sc, size):
        for s in range(2):
            m_prev = m_sc[s]
            m_new = jnp.maximum(m_prev, _sublane_all(jnp.maximum, c_sc[s]))
            alpha = jnp.exp2(m_prev - m_new)
            for r in range(0, size, SOFTMAX_ROWS):
                x = s_sc[s, r:r + SOFTMAX_ROWS, :].reshape(SOFTMAX_ROWS // SUBLANES, SUBLANES, tq) - m_new[None]
                p = jnp.exp2(x)
                p_sc[s, r:r + SOFTMAX_ROWS, :] = p.reshape(SOFTMAX_ROWS, tq).astype(BF16)
            a_sc[s] = alpha
            m_sc[s] = m_new

    def accumulate(p_sc, a_sc, start, size):
        vt = vt_ref[:, pl.ds(start, size)]
        for s in range(2):
            acc = acc_sc[s].reshape(dv // SUBLANES, SUBLANES, tq) * a_sc[s][None]
            acc_sc[s] = acc.reshape(dv, tq) + _dot(vt, p_sc[s, 0:size, :])

    bufs = ((s0_sc, c0_sc, p0_sc, a0_sc), (s1_sc, c1_sc, p1_sc, a1_sc))
    sizes = [KV_CHUNK] * ((t - ROW_BLOCK) // KV_CHUNK - 1) + [KV_CHUNK // 2] * 2
    starts = [ROW_BLOCK + sum(sizes[:c]) for c in range(len(sizes))]
    n_chunks = len(sizes)

    def result(s):
        l = acc_sc[s, LANES:LANES + SUBLANES, :]
        o_t = acc_sc[s, 0:LANES, :].reshape(LANES // SUBLANES, SUBLANES, tq) / l[None]
        return o_t.reshape(LANES, tq).T

    def finish():
        o0 = result(0)
        o1 = result(1)
        if diff:
            lv = lamv_ref[...]
            lam = (jnp.exp(jnp.sum(lv[0:1] * lv[1:2], axis=-1, keepdims=True))
                   - jnp.exp(jnp.sum(lv[2:3] * lv[3:4], axis=-1, keepdims=True)) + lam_init)
            o = o0 - lam * o1
            o_ref[...] = _rms(o, subln_ref[...]) * (1.0 - lam_init)
        else:
            lane = lax.broadcasted_iota(jnp.int32, (1, LANES), 1)
            o_ref[...] = jnp.where(lane < MLA_V, o0, o1)

    m_sc[...] = jnp.full(m_sc.shape, NEG_BIG, F32)
    acc_sc[...] = jnp.zeros(acc_sc.shape, F32)

    @pl.when(i == 0)
    def _():
        scores(s0_sc, c0_sc, 0, ROW_BLOCK)
        softmax(s0_sc, c0_sc, p0_sc, a0_sc, ROW_BLOCK)
        scores(s1_sc, c1_sc, starts[0], sizes[0], qn_ref)
        accumulate(p0_sc, a0_sc, 0, ROW_BLOCK)
        finish()

    @pl.when(i > 0)
    def _():
        for n in range(1, n_chunks + 1):
            s_a, c_a, p_a, a_a = bufs[(n - 1) % 2]
            s_b, c_b, p_b, a_b = bufs[n % 2]
            if n < n_chunks:
                scores(s_a, c_a, starts[n], sizes[n])
            else:
                scores(s_a, c_a, 0, ROW_BLOCK)
            if n > 1:
                accumulate(p_a, a_a, starts[n - 2], sizes[n - 2])
            softmax(s_b, c_b, p_b, a_b, sizes[n - 1])
        s_a, c_a, p_a, a_a = bufs[n_chunks % 2]
        s_b, c_b, p_b, a_b = bufs[(n_chunks + 1) % 2]
        accumulate(p_a, a_a, starts[-1], sizes[-1])
        softmax(s_b, c_b, p_b, a_b, ROW_BLOCK)
        scores(s1_sc, c1_sc, starts[0], sizes[0], qn_ref)
        accumulate(p_b, a_b, 0, ROW_BLOCK)
        finish()


def _flash(q, k, vt, extra, *, diff, lam_init=0.0):
    groups, t = vt.shape[0], vt.shape[2]
    nb = t // ROW_BLOCK
    assert (t - ROW_BLOCK) % (2 * KV_CHUNK) == 0
    if diff:
        q_spec = pl.BlockSpec((None, 2, ROW_BLOCK, LANES), lambda g, i: (g, 0, i, 0))
        qn_spec = pl.BlockSpec((None, 2, ROW_BLOCK, LANES), lambda g, i: (g, 0, jnp.minimum(i + 1, nb - 1), 0))
        k_spec = pl.BlockSpec((None, t, LANES), lambda g, i: (g, 0, 0), pipeline_mode=pl.Buffered(1))
        extra_specs = [pl.BlockSpec(e.shape, lambda g, i: (0, 0)) for e in extra]
    else:
        q_spec = pl.BlockSpec((2, ROW_BLOCK, LANES), lambda g, i: (g, i, 0))
        qn_spec = pl.BlockSpec((2, ROW_BLOCK, LANES), lambda g, i: (g, jnp.minimum(i + 1, nb - 1), 0))
        k_spec = pl.BlockSpec((2, t, LANES), lambda g, i: (g, 0, 0), pipeline_mode=pl.Buffered(1))
        extra_specs = []
    return pl.pallas_call(
        functools.partial(_flash_kernel, diff=diff, lam_init=lam_init),
        grid=(groups, nb),
        in_specs=[q_spec, qn_spec, k_spec,
                  pl.BlockSpec((None, VT_ROWS, t), lambda g, i: (g, 0, 0), pipeline_mode=pl.Buffered(1))] + extra_specs,
        out_specs=pl.BlockSpec((ROW_BLOCK, LANES), lambda g, i: (i, g)),
        out_shape=jax.ShapeDtypeStruct((t, groups * LANES), F32),
        scratch_shapes=[pltpu.VMEM((2, SUBLANES, ROW_BLOCK), F32), pltpu.VMEM((2, VT_ROWS, ROW_BLOCK), F32)]
        + [pltpu.VMEM((2, KV_CHUNK, ROW_BLOCK), F32)] * 2 + [pltpu.VMEM((2, KV_CHUNK, ROW_BLOCK), BF16)] * 2
        + [pltpu.VMEM((2, SUBLANES, ROW_BLOCK), F32)] * 4,
        compiler_params=_params(("parallel", "arbitrary")),
        name="flash_diff" if diff else "flash_mla",
    )(q, q, k, vt, *extra)


NA_ROWS_PER_STEP = ROW_BLOCK // GRID_W


def _na_kernel(q_ref, k_ref, v_ref, bt_ref, o_ref, *, grid_rows):
    i = pl.program_id(1)
    lane = lax.broadcasted_iota(jnp.int32, (1, LANES), 1)
    first = lane < NA_HD
    kctx = k_ref[0:ROW_BLOCK, :]
    vctx = v_ref[0:ROW_BLOCK, :]

    def stacked(q):
        zero = jnp.zeros_like(q)
        return jnp.concatenate([jnp.where(first, q, zero), jnp.where(first, zero, q)], axis=0)

    @pl.when(i == 0)
    def _():
        q2 = stacked(q_ref[...])
        sc = _dot_nt(q2, kctx)
        p = jnp.exp(sc - jnp.max(sc, axis=-1, keepdims=True))
        o = _dot(p.astype(BF16), vctx) / jnp.sum(p, axis=-1, keepdims=True)
        o_ref[...] = jnp.where(first, o[0:ROW_BLOCK], o[ROW_BLOCK:])

    @pl.when(i > 0)
    def _():
        nkeys = WIN_R * GRID_W
        starts, s_lat, s_ctx = [], [], []
        for rr in range(NA_ROWS_PER_STEP):
            r = (i - 1) * NA_ROWS_PER_STEP + rr
            r0 = jnp.clip(r - WIN_R // 2, 0, grid_rows - WIN_R)
            start = pl.multiple_of(ROW_BLOCK + r0 * GRID_W, GRID_W)
            q2 = stacked(q_ref[rr * GRID_W:(rr + 1) * GRID_W, :])
            starts.append(start)
            s_lat.append(_dot_nt(q2, k_ref[pl.ds(start, nkeys), :]) + bt_ref[r0 - r + WIN_R - 1])
            s_ctx.append(_dot_nt(q2, kctx))
        p_lat, p_ctx, denom = [], [], []
        for rr in range(NA_ROWS_PER_STEP):
            m = jnp.maximum(jnp.max(s_lat[rr], axis=-1, keepdims=True), jnp.max(s_ctx[rr], axis=-1, keepdims=True))
            pl_, pc_ = jnp.exp(s_lat[rr] - m), jnp.exp(s_ctx[rr] - m)
            denom.append(jnp.sum(pl_, axis=-1, keepdims=True) + jnp.sum(pc_, axis=-1, keepdims=True))
            p_lat.append(pl_.astype(BF16))
            p_ctx.append(pc_.astype(BF16))
        for rr in range(NA_ROWS_PER_STEP):
            o = (_dot(p_lat[rr], v_ref[pl.ds(starts[rr], nkeys), :]) + _dot(p_ctx[rr], vctx)) / denom[rr]
            o_ref[rr * GRID_W:(rr + 1) * GRID_W, :] = jnp.where(first, o[0:GRID_W], o[GRID_W:])


def _na_attention(q, k, v, bias):
    npair, t = q.shape[0], q.shape[1]
    nb = t // ROW_BLOCK
    grid_rows = (t - ROW_BLOCK) // GRID_W
    resident = pl.BlockSpec((None, t, LANES), lambda p, i: (p, 0, 0), pipeline_mode=pl.Buffered(1))
    return pl.pallas_call(
        functools.partial(_na_kernel, grid_rows=grid_rows),
        grid=(npair, nb),
        in_specs=[pl.BlockSpec((None, ROW_BLOCK, LANES), lambda p, i: (p, i, 0)), resident, resident,
                  pl.BlockSpec((None,) + bias.shape[1:], lambda p, i: (p, 0, 0, 0))],
        out_specs=pl.BlockSpec((ROW_BLOCK, LANES), lambda p, i: (i, p)),
        out_shape=jax.ShapeDtypeStruct((t, npair * LANES), F32),
        compiler_params=_params(("parallel", "arbitrary")),
        name="na_attention",
    )(q, k, v, bias)


def _na_bias_table(rpb):
    w = np.arange(GRID_W)
    c0 = np.clip(w - WIN_C // 2, 0, GRID_W - WIN_C)
    inside = (w[None, :] >= c0[:, None]) & (w[None, :] < c0[:, None] + WIN_C)
    edge = GRID_W - WIN_C
    pad = jnp.pad(rpb, ((0, 0), (0, 0), (edge, edge)))
    col = jnp.stack([pad[:, :, GRID_W - 1 - wq:2 * GRID_W - 1 - wq] for wq in range(GRID_W)], axis=2)
    col = jnp.where(inside[None, None], col, NEG_BIG)
    tab = jnp.stack([col[:, var:var + WIN_R] for var in range(WIN_R)], axis=1)
    tab = tab.transpose(0, 1, 3, 2, 4).reshape(NA_HEADS // 2, 2, WIN_R, GRID_W, WIN_R * GRID_W)
    return tab.transpose(0, 2, 1, 3, 4).reshape(NA_HEADS // 2, WIN_R, 2 * GRID_W, WIN_R * GRID_W)


def _out_kernel(*refs, final):
    if final:
        a_ref, b_ref, g_ref, wg_ref, w_ref, x_ref, mod_ref, fg_ref, o_ref = refs
    else:
        a_ref, b_ref, g_ref, wg_ref, w_ref, x_ref, mod_ref, o_ref = refs
    half = a_ref.shape[-1]
    d = x_ref.shape[-1]
    hb = _norm_mod(x_ref[...], g_ref[...], mod_ref[...]).astype(BF16)
    gate = _dot(hb, wg_ref[...])
    mix = jnp.concatenate([a_ref[...], b_ref[...]], axis=1)
    y = _dot((mix * (gate * jax.nn.sigmoid(gate))).astype(BF16), w_ref[...])
    x_new = x_ref[...] + mod_ref[:, 2 * d:3 * d] * y
    o_ref[...] = _rms(x_new, fg_ref[...]) if final else x_new


def _out_proj(a, b, norm_g, w_gate, w, x, mod, final_g=None):
    t, d = x.shape
    nb = t // ROW_BLOCK
    final = final_g is not None
    rows = lambda c: pl.BlockSpec((ROW_BLOCK, c), lambda i: (i, 0))
    in_specs = [rows(a.shape[1]), rows(b.shape[1]), pl.BlockSpec(norm_g.shape, lambda i: (0, 0)),
                pl.BlockSpec(w_gate.shape, lambda i: (0, 0)), pl.BlockSpec(w.shape, lambda i: (0, 0)), rows(d),
                pl.BlockSpec((None, 1, 3 * d), lambda i: (jnp.minimum(i, 1), 0, 0))]
    args = [a, b, norm_g, w_gate, w, x, mod]
    if final:
        in_specs.append(pl.BlockSpec((1, d), lambda i: (0, 0)))
        args.append(final_g)
    return pl.pallas_call(
        functools.partial(_out_kernel, final=final),
        grid=(nb,),
        in_specs=in_specs,
        out_specs=pl.BlockSpec((ROW_BLOCK, d), lambda i: (jnp.maximum(i - 1, 0), 0)) if final else rows(d),
        out_shape=jax.ShapeDtypeStruct((t - ROW_BLOCK if final else t, d), F32),
        compiler_params=_params(("arbitrary",) if final else ("parallel",)),
        name="out_proj_final" if final else "out_proj",
    )(*args)


def _lru_kernel(*refs, reverse, nb):
    if reverse:
        (x_ref, xp_ref, xn_ref, cw_ref, cb_ref, lam_ref, wa_ref, ba_ref, wx_ref, bx_ref, hf_ref,
         o_ref, xs_sc, a_sc, b_sc, carry_sc) = refs
    else:
        (x_ref, xp_ref, xn_ref, cw_ref, cb_ref, lam_ref, wa_ref, ba_ref, wx_ref, bx_ref,
         o_ref, xs_sc, a_sc, b_sc, carry_sc) = refs
    step = pl.program_id(0)
    blk = jnp.where(step == 0, 0, nb - step) if reverse else step
    tb = x_ref.shape[0]

    @pl.when(step == 0)
    def _():
        carry_sc[...] = jnp.zeros_like(carry_sc)

    prev_ok = blk >= 2
    next_ok = (blk >= 1) & (blk < nb - 1)
    xs_sc[0:SUBLANES, :] = jnp.where(prev_ok, xp_ref[...], 0.0)
    xs_sc[SUBLANES:SUBLANES + tb, :] = x_ref[...]
    xs_sc[SUBLANES + tb:2 * SUBLANES + tb, :] = jnp.where(next_ok, xn_ref[...], 0.0)
    u = cb_ref[...]
    for j in range(CONV_W):
        off = SUBLANES + j - CONV_W // 2
        u = u + xs_sc[off:off + tb, :] * cw_ref[j:j + 1, :]

    ub = u.astype(BF16)
    r = jax.nn.sigmoid(_dot(ub, wa_ref[...]) + ba_ref[...])
    gi = jax.nn.sigmoid(_dot(ub, wx_ref[...]) + bx_ref[...])
    nl = -lam_ref[...]
    softplus = jnp.maximum(nl, 0.0) + jnp.log1p(jnp.exp(-jnp.abs(nl)))
    log_a = -LRU_C * r * softplus
    a_sc[...] = jnp.exp(log_a)
    b_sc[...] = jnp.sqrt(1.0 - jnp.exp(2.0 * log_a)) * (gi * u)

    row = lax.broadcasted_iota(jnp.int32, (SUBLANES, 1), 0)
    nchunk = tb // SUBLANES

    def chunk(c, carry):
        cc = nchunk - 1 - c if reverse else c
        start = pl.multiple_of(cc * SUBLANES, SUBLANES)
        a = a_sc[pl.ds(start, SUBLANES), :]
        b = b_sc[pl.ds(start, SUBLANES), :]
        for k in (1, 2, 4):
            shift = SUBLANES - k if reverse else k
            valid = (row < SUBLANES - k) if reverse else (row >= k)
            a_sh = pltpu.roll(a, shift, 0)
            b_sh = pltpu.roll(b, shift, 0)
            b = jnp.where(valid, a * b_sh + b, b)
            a = jnp.where(valid, a * a_sh, a)
        h = b + a * carry
        if reverse:
            o_ref[pl.ds(start, SUBLANES), :] = h + hf_ref[pl.ds(start, SUBLANES), :]
            new = h[0:1, :]
        else:
            o_ref[pl.ds(start, SUBLANES), :] = h
            new = h[SUBLANES - 1:SUBLANES, :]
        return jnp.broadcast_to(new, carry.shape)

    carry_sc[...] = lax.fori_loop(0, nchunk, chunk, carry_sc[...], unroll=True)


def _lru_pass(x, conv_w, conv_b, lam, wa, ba, wx, bx, hf, *, reverse):
    t, c = x.shape
    tb = ROW_BLOCK
    nb = t // tb
    per8 = tb // SUBLANES
    nb8 = t // SUBLANES
    if reverse:
        blk = lambda s: jnp.where(s == 0, 0, nb - s)
    else:
        blk = lambda s: s
    full = lambda a: pl.BlockSpec(a.shape, lambda s: (0,) * a.ndim)
    main = pl.BlockSpec((tb, c), lambda s: (blk(s), 0))
    args = [x, x, x, conv_w, conv_b, lam, wa, ba, wx, bx]
    in_specs = [main,
                pl.BlockSpec((SUBLANES, c), lambda s: (jnp.maximum(blk(s) * per8 - 1, 0), 0)),
                pl.BlockSpec((SUBLANES, c), lambda s: (jnp.minimum((blk(s) + 1) * per8, nb8 - 1), 0)),
                full(conv_w), full(conv_b), full(lam), full(wa), full(ba), full(wx), full(bx)]
    if reverse:
        args.append(hf)
        in_specs.append(main)
    return pl.pallas_call(
        functools.partial(_lru_kernel, reverse=reverse, nb=nb),
        grid=(nb,),
        in_specs=in_specs,
        out_specs=main,
        out_shape=jax.ShapeDtypeStruct((t, c), F32),
        scratch_shapes=[pltpu.VMEM((tb + 2 * SUBLANES, c), F32), pltpu.VMEM((tb, c), F32),
                        pltpu.VMEM((tb, c), F32), pltpu.VMEM((SUBLANES, c), F32)],
        compiler_params=_params(("arbitrary",)),
        name="lru_bwd" if reverse else "lru_fwd",
    )(*args)


def _block_diag(w):
    g, bw, _ = w.shape
    eye = jnp.eye(g, dtype=w.dtype)
    return (eye[:, None, :, None] * w[:, :, None, :]).reshape(g * bw, g * bw)


def _rope_tables(n, ctx_len, rot_dim, lead, tail, repeat):
    t = jnp.arange(n)
    row = (t // GRID_W).astype(F32)
    col = (t % GRID_W).astype(F32)
    n_freq = rot_dim // 4
    inv = ROPE_BASE ** (-jnp.arange(n_freq, dtype=F32) / n_freq)
    ang = jnp.concatenate([row[:, None] * inv, col[:, None] * inv], axis=-1)
    cos, sin = jnp.cos(ang), jnp.sin(ang)
    cos_l = jnp.concatenate([jnp.ones((n, lead), F32)] + [cos, cos] * repeat + [jnp.ones((n, tail), F32)], axis=-1)
    sin_l = jnp.concatenate([jnp.zeros((n, lead), F32)] + [-sin, sin] * repeat + [jnp.zeros((n, tail), F32)], axis=-1)
    cos_t = jnp.concatenate([jnp.ones((ctx_len, LANES), F32), cos_l], axis=0)
    sin_t = jnp.concatenate([jnp.zeros((ctx_len, LANES), F32), sin_l], axis=0)
    return cos_t, sin_t


def _even_weights(w_in, w_uq, w_ukv):
    d = w_in.shape[0]
    q, k, v, g_na, lq, lkv, kr, g_mla = jnp.split(
        w_in, [512, 1024, 1536, 2048, 2048 + Q_LORA, 2048 + Q_LORA + KV_LORA, 2048 + Q_LORA + KV_LORA + MLA_ROPE], axis=1)
    pad = LANES - MLA_QK
    kr_pad = jnp.concatenate([jnp.zeros((d, MLA_NOPE), F32), kr, jnp.zeros((d, pad), F32)], axis=1)
    w = jnp.concatenate([q * (NA_HD ** -0.5), k, v, lq, lkv, kr_pad], axis=1).astype(BF16)
    w_gate = jnp.concatenate([g_na, g_mla], axis=1).astype(BF16)
    uq = w_uq.reshape(Q_LORA, MLA_HEADS, MLA_QK)
    zq = jnp.zeros((Q_LORA, MLA_HEADS, pad), F32)
    wuq = jnp.concatenate([uq, zq], axis=-1).reshape(Q_LORA, -1).astype(BF16)
    ukv = w_ukv.reshape(KV_LORA, MLA_HEADS, MLA_NOPE + MLA_V)
    k_nope = jnp.concatenate([ukv[..., :MLA_NOPE], jnp.zeros((KV_LORA, MLA_HEADS, LANES - MLA_NOPE), F32)], axis=-1)
    wkv = jnp.concatenate([k_nope.reshape(KV_LORA, -1), ukv[..., MLA_NOPE:].reshape(KV_LORA, -1)], axis=1).astype(BF16)
    return w, w_gate, wuq, wkv


def _odd_weights(w_in):
    q, k, v, g_d, u, g_lru = jnp.split(w_in, [512, 1024, 1536, 2048, 2560], axis=1)
    w = jnp.concatenate([q * (DIFF_HD ** -0.5), k, v, u], axis=1).astype(BF16)
    return w, jnp.concatenate([g_d, g_lru], axis=1).astype(BF16)


def kernel(x, c, ctx, c_ctx, mod_w, mod_b, norm_g, final_g, e_w_in, e_w_out, na_rpb, mla_q_norm, mla_w_uq, mla_kv_norm, mla_w_ukv, o_w_in, o_w_out, diff_lq1, diff_lk1, diff_lq2, diff_lk2, diff_subln, lru_conv_w, lru_conv_b, lru_lambda, lru_wa, lru_ba, lru_wx, lru_bx):
    batch, n, d = x.shape
    ctx_len = ctx.shape[1]
    assert batch == 1 and ctx_len == ROW_BLOCK and d == D_MODEL
    assert n // GRID_W >= WIN_R

    cvec = jnp.concatenate([c_ctx[None], c, jnp.zeros((SUBLANES - 2, d), F32)], axis=0)
    mods = _modulation(cvec, mod_w, mod_b)[:, 0:2].reshape(DEPTH, 2, 1, 3 * d)

    cos_m, sin_m = _rope_tables(n, ctx_len, MLA_ROPE, MLA_NOPE, LANES - MLA_QK, 1)
    cos_d, sin_d = _rope_tables(n, ctx_len, DIFF_HD, 0, 0, 2)

    xs = jnp.concatenate([ctx[0], x[0]], axis=0)
    for l in range(DEPTH):
        i = l // 2
        g_row = norm_g[l][None]
        if l % 2 == 0:
            w, w_gate, wuq, wkv = _even_weights(e_w_in[i], mla_w_uq[i], mla_w_ukv[i])
            naq, nak, nav, mq, mk, mv = _in_even(
                xs, g_row, mods[l], w, mla_q_norm[i][None], wuq, mla_kv_norm[i][None], wkv, cos_m, sin_m)
            a = _na_attention(naq, nak, nav, _na_bias_table(na_rpb[i]))
            b = _flash(mq, mk, mv, (), diff=False)
            w_out = e_w_out[i]
        else:
            lam_init = 0.8 - 0.6 * math.exp(-0.3 * l)
            w, w_gate = _odd_weights(o_w_in[i])
            dq, dk, dv, u = _in_odd(xs, g_row, mods[l], w, cos_d, sin_d)
            lamv = jnp.stack([diff_lq1[i], diff_lk1[i], diff_lq2[i], diff_lk2[i]])
            a = _flash(dq, dk, dv, (lamv, diff_subln[i][None]), diff=True, lam_init=lam_init)
            lru = lambda dr, hf: _lru_pass(
                u, lru_conv_w[i], lru_conv_b[i][None], lru_lambda[i, dr][None],
                _block_diag(lru_wa[i, dr]).astype(BF16), lru_ba[i, dr][None],
                _block_diag(lru_wx[i, dr]).astype(BF16), lru_bx[i, dr][None], hf, reverse=dr == 1)
            b = lru(1, lru(0, None))
            w_out = o_w_out[i]
        xs = _out_proj(a, b, g_row, w_gate, w_out.astype(BF16), xs, mods[l], final_g[None] if l == DEPTH - 1 else None)
    return xs[None]
```
